```python
import math
import jax, jax.numpy as jnp
from jax import lax
import numpy as np

D_MODEL = 1024
BATCH = 8
SEQ = 2048
DEPTH = 4

D_MIX = D_MODEL
N_MIXERS = 4
W_GRP = D_MIX // N_MIXERS
SGU_HEADS = 4
SGU_HEAD_DIM = W_GRP // SGU_HEADS
CHUNK = 128
POOL_WINDOWS = (2, 4, 8, 16)
POOL_GROUPS = len(POOL_WINDOWS)
POOL_GROUP_DIM = W_GRP // POOL_GROUPS
CONV_WIDTH = 3
S5_GROUP_CH = 16
S5_GROUPS = W_GRP // S5_GROUP_CH
S5_STATE = 64
P_IN = 2 * W_GRP + W_GRP + 3 * W_GRP + W_GRP
D_FF = ((8 * D_MODEL // 3 + 255) // 256) * 256
N_ADA = 9
EPS = 1e-6

kernel_name = "hybrid_parallel_mixer_trunk"


def rmsnorm(x, g):
    xf = x.astype(jnp.float32)
    y = xf * lax.rsqrt(jnp.mean(xf * xf, axis=-1, keepdims=True) + EPS)
    return (y * g.astype(jnp.float32)).astype(x.dtype)


def group_rmsnorm(y, g):
    b, s, _ = y.shape
    yf = y.astype(jnp.float32).reshape(b, s, N_MIXERS, W_GRP)
    yf = yf * lax.rsqrt(jnp.mean(yf * yf, axis=-1, keepdims=True) + EPS)
    return (yf.reshape(b, s, D_MIX) * g.astype(jnp.float32)).astype(y.dtype)


def modulate(h, shift, scale):
    return h * (1.0 + scale) + shift


def swiglu(h, w_in, w_out):
    a, b = jnp.split(h @ w_in, 2, axis=-1)
    return (jax.nn.silu(a) * b) @ w_out


def sgu_mixer(z, w_s, b_s):
    bsz, s, _ = z.shape
    z = jax.nn.gelu(z)
    u, v = jnp.split(z, 2, axis=-1)
    v = v.reshape(bsz, s // CHUNK, CHUNK, SGU_HEADS, SGU_HEAD_DIM)
    vf = v.astype(jnp.float32)
    mu = jnp.mean(vf, axis=-1, keepdims=True)
    var = jnp.mean(jnp.square(vf - mu), axis=-1, keepdims=True)
    v = ((vf - mu) * lax.rsqrt(var + EPS)).astype(z.dtype)
    mask = jnp.tril(jnp.ones((CHUNK, CHUNK), dtype=w_s.dtype))
    mixed = jnp.einsum('hts,bnshd->bnthd', w_s * mask, v)
    mixed = mixed + b_s.T[None, None, :, :, None]
    return u * mixed.reshape(bsz, s, W_GRP)


def pool_mixer(z, w_p, scale):
    bsz, s, _ = z.shape
    zf = z.astype(jnp.float32).reshape(bsz, s, POOL_GROUPS, POOL_GROUP_DIM)
    cs = jnp.concatenate([jnp.zeros_like(zf[:, :1]), jnp.cumsum(zf, axis=1)], axis=1)
    t = jnp.arange(s)
    win = jnp.array(POOL_WINDOWS, dtype=jnp.int32)
    lo = jnp.maximum(t[:, None] + 1 - win[None, :], 0)
    cnt = (t[:, None] + 1 - lo).astype(jnp.float32)
    lower = cs[:, lo, jnp.arange(POOL_GROUPS)[None, :]]
    mean = (cs[:, 1:] - lower) / cnt[None, :, :, None]
    p = (mean - zf).astype(z.dtype)
    out = jnp.einsum('bsgc,gcd->bsgd', p, w_p).reshape(bsz, s, W_GRP)
    return out * scale


def conv_mixer(z, conv_w):
    bg, cg, xh = jnp.split(z, 3, axis=-1)
    y = cg * xh
    y = lax.conv_general_dilated(
        y, conv_w[:, None, :], window_strides=(1,), padding=[(CONV_WIDTH - 1, 0)],
        dimension_numbers=('NWC', 'WIO', 'NWC'), feature_group_count=W_GRP)
    return bg * y


def s5_mixer(u, lam_re, lam_im, b_re, b_im, c_re, c_im, d, log_dt, glu_w, glu_b):
    bsz, s, _ = u.shape
    f32 = jnp.float32
    dt = jnp.exp(log_dt.astype(f32))[:, None]
    lre, lim = lam_re.astype(f32), lam_im.astype(f32)
    mag = jnp.exp(lre * dt)
    ang = lim * dt
    a_re, a_im = mag * jnp.cos(ang), mag * jnp.sin(ang)
    nr, ni = a_re - 1.0, a_im
    den = lre * lre + lim * lim
    k_re = (nr * lre + ni * lim) / den
    k_im = (ni * lre - nr * lim) / den
    br, bi = b_re.astype(f32), b_im.astype(f32)
    bb_re = k_re[..., None] * br - k_im[..., None] * bi
    bb_im = k_re[..., None] * bi + k_im[..., None] * br
    uf = u.astype(f32)
    ug = uf.reshape(bsz, s, S5_GROUPS, S5_GROUP_CH)
    bu_re = jnp.einsum('bsgc,gpc->bsgp', ug, bb_re)
    bu_im = jnp.einsum('bsgc,gpc->bsgp', ug, bb_im)
    ar = jnp.broadcast_to(a_re, bu_re.shape)
    ai = jnp.broadcast_to(a_im, bu_re.shape)

    def combine(e1, e2):
        a1r, a1i, b1r, b1i = e1
        a2r, a2i, b2r, b2i = e2
        return (a2r * a1r - a2i * a1i,
                a2r * a1i + a2i * a1r,
                a2r * b1r - a2i * b1i + b2r,
                a2r * b1i + a2i * b1r + b2i)

    _, _, xr, xi = lax.associative_scan(combine, (ar, ai, bu_re, bu_im), axis=1)
    y = (jnp.einsum('gcp,bsgp->bsgc', c_re.astype(f32), xr)
         - jnp.einsum('gcp,bsgp->bsgc', c_im.astype(f32), xi))
    y = y.reshape(bsz, s, W_GRP) + d.astype(f32) * uf
    y = jax.nn.gelu(y).astype(u.dtype)
    return y * jax.nn.sigmoid(y @ glu_w + glu_b)


def setup_inputs(seed: int = 0) -> dict:
    key = jax.random.key(seed)
    ks = jax.random.split(key, 32)
    f32 = jnp.float32
    L, D = DEPTH, D_MODEL

    def nrm(k, shape, scale):
        return jax.random.normal(k, shape, f32) * scale

    def gain(k, shape):
        return 1.0 + 0.05 * jax.random.normal(k, shape, f32)

    lam_im0 = jnp.broadcast_to(math.pi * jnp.arange(S5_STATE, dtype=f32), (L, S5_GROUPS, S5_STATE))
    return {
        "x": nrm(ks[0], (BATCH, SEQ, D), 1.0),
        "c": nrm(ks[1], (BATCH, D), 1.0),
        "ada_w": nrm(ks[2], (L, D, N_ADA * D), 0.5 * D ** -0.5),
        "ada_b": nrm(ks[3], (L, N_ADA * D), 0.01),
        "norm1_g": gain(ks[4], (L, D)),
        "ffn1_w_in": nrm(ks[5], (L, D, 2 * D_FF), D ** -0.5),
        "ffn1_w_out": nrm(ks[6], (L, D_FF, D), D_FF ** -0.5),
        "norm2_g": gain(ks[7], (L, D)),
        "w_mix_in": nrm(ks[8], (L, D, P_IN), D ** -0.5),
        "sgu_w": nrm(ks[9], (L, SGU_HEADS, CHUNK, CHUNK), CHUNK ** -0.5),
        "sgu_b": 1.0 + nrm(ks[10], (L, SGU_HEADS, CHUNK), 0.1),
        "pool_w": nrm(ks[11], (L, POOL_GROUPS, POOL_GROUP_DIM, POOL_GROUP_DIM), POOL_GROUP_DIM ** -0.5),
        "pool_scale": 1.0 + nrm(ks[12], (L, W_GRP), 0.1),
        "conv_w": nrm(ks[13], (L, CONV_WIDTH, W_GRP), CONV_WIDTH ** -0.5),
        "s5_lambda_re": -0.5 + nrm(ks[14], (L, S5_GROUPS, S5_STATE), 0.01),
        "s5_lambda_im": lam_im0 + nrm(ks[15], (L, S5_GROUPS, S5_STATE), 0.01),
        "s5_b_re": nrm(ks[16], (L, S5_GROUPS, S5_STATE, S5_GROUP_CH), (2 * S5_GROUP_CH) ** -0.5),
        "s5_b_im": nrm(ks[17], (L, S5_GROUPS, S5_STATE, S5_GROUP_CH), (2 * S5_GROUP_CH) ** -0.5),
        "s5_c_re": nrm(ks[18], (L, S5_GROUPS, S5_GROUP_CH, S5_STATE), (2 * S5_STATE) ** -0.5),
        "s5_c_im": nrm(ks[19], (L, S5_GROUPS, S5_GROUP_CH, S5_STATE), (2 * S5_STATE) ** -0.5),
        "s5_d": nrm(ks[20], (L, W_GRP), 1.0),
        "s5_log_dt": jax.random.uniform(ks[21], (L, S5_GROUPS), f32, math.log(1e-3), math.log(1e-1)),
        "s5_glu_w": nrm(ks[22], (L, W_GRP, W_GRP), W_GRP ** -0.5),
        "s5_glu_b": nrm(ks[23], (L, W_GRP), 0.01),
        "mix_norm_g": gain(ks[24], (L, D_MIX)),
        "w_mix_out": nrm(ks[25], (L, D_MIX, D), D_MIX ** -0.5),
        "norm3_g": gain(ks[26], (L, D)),
        "ffn2_w_in": nrm(ks[27], (L, D, 2 * D_FF), D ** -0.5),
        "ffn2_w_out": nrm(ks[28], (L, D_FF, D), D_FF ** -0.5),
        "final_norm_g": gain(ks[29], (D,)),
    }


def reference(x, c, ada_w, ada_b, norm1_g, ffn1_w_in, ffn1_w_out, norm2_g, w_mix_in,
              sgu_w, sgu_b, pool_w, pool_scale, conv_w, s5_lambda_re, s5_lambda_im,
              s5_b_re, s5_b_im, s5_c_re, s5_c_im, s5_d, s5_log_dt, s5_glu_w, s5_glu_b,
              mix_norm_g, w_mix_out, norm3_g, ffn2_w_in, ffn2_w_out, final_norm_g):
    c_act = jax.nn.silu(c)
    for l in range(DEPTH):
        cond = (c_act @ ada_w[l] + ada_b[l])[:, None, :]
        sh1, sc1, g1, sh2, sc2, g2, sh3, sc3, g3 = jnp.split(cond, N_ADA, axis=-1)

        h = modulate(rmsnorm(x, norm1_g[l]), sh1, sc1)
        x = x + 0.5 * g1 * swiglu(h, ffn1_w_in[l], ffn1_w_out[l])

        h = modulate(rmsnorm(x, norm2_g[l]), sh2, sc2)
        z = h @ w_mix_in[l]
        za, zb, zc, zd = jnp.split(z, [2 * W_GRP, 3 * W_GRP, 6 * W_GRP], axis=-1)
        ya = sgu_mixer(za, sgu_w[l], sgu_b[l])
        yb = pool_mixer(zb, pool_w[l], pool_scale[l])
        yc = conv_mixer(zc, conv_w[l])
        yd = s5_mixer(zd, s5_lambda_re[l], s5_lambda_im[l], s5_b_re[l], s5_b_im[l],
                      s5_c_re[l], s5_c_im[l], s5_d[l], s5_log_dt[l], s5_glu_w[l], s5_glu_b[l])
        y = group_rmsnorm(jnp.concatenate([ya, yb, yc, yd], axis=-1), mix_norm_g[l])
        x = x + g2 * (y @ w_mix_out[l])

        h = modulate(rmsnorm(x, norm3_g[l]), sh3, sc3)
        x = x + 0.5 * g3 * swiglu(h, ffn2_w_in[l], ffn2_w_out[l])
    return rmsnorm(x, final_norm_g)
```

```python
import functools
import math

import jax
import jax.numpy as jnp
from jax import lax
from jax.experimental import pallas as pl
from jax.experimental.pallas import tpu as pltpu

F32 = jnp.float32
BF16 = jnp.bfloat16

D_MODEL = 1024
BATCH = 8
SEQ = 2048
DEPTH = 4
W_GRP = 256
SGU_HEADS = 4
SGU_HEAD_DIM = 64
CHUNK = 128
POOL_WINDOWS = (2, 4, 8, 16)
POOL_GROUP_DIM = 64
S5_GROUP_CH = 16
S5_GROUPS = 16
S5_STATE = 64
N_STATE = S5_GROUPS * S5_STATE
P_IN = 7 * W_GRP
D_FF = 2816
N_ADA = 9
EPS = 1e-6

N_ROWS = SEQ * BATCH
SUBLANES = 8
LANES = 128
VMEM_LIMIT = 56 * 1024 * 1024

FFN_ROWS = 512
FFN_CHUNK = 256
MIX_STEPS = CHUNK
MIX_ROWS = MIX_STEPS * BATCH
POOL_HALO = 16 * BATCH
CONV_HALO = 2 * BATCH
ADA_COLS = 1152


def _dot(a, b):
    return jnp.dot(a, b, preferred_element_type=F32)


def _split_bf16(x):
    hi = x.astype(BF16)
    lo = (x - hi.astype(F32)).astype(BF16)
    return hi, lo


def _rmsnorm(x, g):
    ms = jnp.mean(x * x, axis=-1, keepdims=True)
    return x * lax.rsqrt(ms + EPS) * g


def _per_batch(x, fn):
    rows, c = x.shape
    return fn(x.reshape(rows // BATCH, BATCH, c)).reshape(rows, c)


def _modulate(y, shift, scale):
    return _per_batch(y, lambda y3: y3 * (1.0 + scale)[None] + shift[None])


def _ada_body(c_ref, w_ref, b_ref, o_ref):
    c = c_ref[...]
    c_act = c * jax.nn.sigmoid(c)
    c_hi, c_lo = _split_bf16(c_act)
    w_hi, w_lo = _split_bf16(w_ref[...])
    o_ref[...] = _dot(c_hi, w_hi) + (_dot(c_lo, w_hi) + _dot(c_hi, w_lo)) + b_ref[...]


def _ada_call(c, ada_w, ada_b):
    n_col = N_ADA * D_MODEL // ADA_COLS
    return pl.pallas_call(
        _ada_body,
        grid=(DEPTH, n_col),
        in_specs=[
            pl.BlockSpec((BATCH, D_MODEL), lambda l, j: (0, 0)),
            pl.BlockSpec((None, D_MODEL, ADA_COLS), lambda l, j: (l, 0, j)),
            pl.BlockSpec((None, 1, ADA_COLS), lambda l, j: (l, 0, j)),
        ],
        out_specs=pl.BlockSpec((None, BATCH, ADA_COLS), lambda l, j: (l, 0, j)),
        out_shape=jax.ShapeDtypeStruct((DEPTH, BATCH, N_ADA * D_MODEL), F32),
        compiler_params=pltpu.CompilerParams(
            dimension_semantics=("arbitrary", "arbitrary"), vmem_limit_bytes=VMEM_LIMIT),
        name="ada_cond",
    )(c, ada_w, ada_b.reshape(DEPTH, 1, N_ADA * D_MODEL))


def _s5_disc_body(lre_ref, lim_ref, ldt_ref, br_ref, bi_ref, are_ref, aim_ref, bbr_ref, bbi_ref):
    lre, lim = lre_ref[...], lim_ref[...]
    dt = jnp.exp(ldt_ref[...])
    mag = jnp.exp(lre * dt)
    ang = lim * dt
    a_re, a_im = mag * jnp.cos(ang), mag * jnp.sin(ang)
    nr, ni = a_re - 1.0, a_im
    den = lre * lre + lim * lim
    k_re = (nr * lre + ni * lim) / den
    k_im = (ni * lre - nr * lim) / den
    br, bi = br_ref[...], bi_ref[...]
    are_ref[...] = a_re
    aim_ref[...] = a_im
    bbr_ref[...] = k_re * br - k_im * bi
    bbi_ref[...] = k_re * bi + k_im * br


def _s5_disc_call(lam_re, lam_im, log_dt, b_re, b_im):
    row = lambda a: a.reshape(DEPTH, 1, N_STATE)
    chan_major = lambda b: jnp.transpose(b, (0, 3, 1, 2)).reshape(DEPTH, S5_GROUP_CH, N_STATE)
    ldt = jnp.repeat(log_dt, S5_STATE, axis=-1)
    vec = pl.BlockSpec((None, 1, N_STATE), lambda l: (l, 0, 0))
    mat = pl.BlockSpec((None, S5_GROUP_CH, N_STATE), lambda l: (l, 0, 0))
    return pl.pallas_call(
        _s5_disc_body,
        grid=(DEPTH,),
        in_specs=[vec, vec, vec, mat, mat],
        out_specs=[vec, vec, mat, mat],
        out_shape=[jax.ShapeDtypeStruct((DEPTH, 1, N_STATE), F32)] * 2
        + [jax.ShapeDtypeStruct((DEPTH, S5_GROUP_CH, N_STATE), F32)] * 2,
        name="s5_discretise",
    )(row(lam_re), row(lam_im), row(ldt), chan_major(b_re), chan_major(b_im))


def _ffn_body(x_ref, mod_ref, g_ref, win_ref, wout_ref, fg_ref, o_ref, *, final_norm):
    x = x_ref[...]
    h = _modulate(_rmsnorm(x, g_ref[...]), mod_ref[0], mod_ref[1]).astype(BF16)
    acc = None
    for j in range(D_FF // FFN_CHUNK):
        lo = j * FFN_CHUNK
        a = _dot(h, win_ref[:, lo:lo + FFN_CHUNK])
        b = _dot(h, win_ref[:, D_FF + lo:D_FF + lo + FFN_CHUNK])
        gated = (a * jax.nn.sigmoid(a) * b).astype(BF16)
        part = _dot(gated, wout_ref[lo:lo + FFN_CHUNK, :])
        acc = part if acc is None else acc + part
    half_gate = 0.5 * mod_ref[2]
    y = x + _per_batch(acc, lambda a3: a3 * half_gate[None])
    if final_norm:
        y = _rmsnorm(y, fg_ref[...])
    o_ref[...] = y


def _ffn_call(x, mod, g, w_in, w_out, final_g, *, final_norm):
    const = lambda shape: pl.BlockSpec(shape, lambda i: (0,) * len(shape),
                                       pipeline_mode=pl.Buffered(1))
    rows = pl.BlockSpec((FFN_ROWS, D_MODEL), lambda i: (i, 0))
    return pl.pallas_call(
        functools.partial(_ffn_body, final_norm=final_norm),
        grid=(N_ROWS // FFN_ROWS,),
        in_specs=[rows, const((3, BATCH, D_MODEL)), const((1, D_MODEL)),
                  const((D_MODEL, 2 * D_FF)), const((D_FF, D_MODEL)), const((1, D_MODEL))],
        out_specs=rows,
        out_shape=jax.ShapeDtypeStruct((N_ROWS, D_MODEL), F32),
        input_output_aliases={0: 0},
        compiler_params=pltpu.CompilerParams(
            dimension_semantics=("arbitrary",), vmem_limit_bytes=VMEM_LIMIT),
        name="ffn_final" if final_norm else "ffn",
    )(x, mod, g, w_in, w_out, final_g)


def _group_norm(y, g):
    ms = jnp.mean(y * y, axis=-1, keepdims=True)
    return (y * lax.rsqrt(ms + EPS) * g).astype(BF16)


def _sgu(za, seg_mean, sgu_w, sgu_bias, v_slab, m_slab):
    z = jax.nn.gelu(za)
    u, v = z[:, :W_GRP], z[:, W_GRP:]
    seg = lambda t: sum(_dot(p, seg_mean) for p in _split_bf16(t))
    d = v - seg(v)
    vn = d * lax.rsqrt(seg(d * d) + EPS)
    n_slab = W_GRP // LANES
    for s in range(n_slab):
        v_slab[s] = vn[:, s * LANES:(s + 1) * LANES]
    lane = lax.broadcasted_iota(jnp.int32, (CHUNK, LANES), 1)
    first_head = lane < SGU_HEAD_DIM
    for s in range(n_slab):
        cols = [v_slab[s, pl.ds(b, CHUNK, stride=BATCH), :] for b in range(BATCH)]
        vt = jnp.concatenate(cols, axis=1)
        keep = jnp.concatenate([first_head] * BATCH, axis=1)
        rhs = jnp.concatenate([jnp.where(keep, vt, 0.0), jnp.where(keep, 0.0, vt)], axis=0)
        mixed = _dot(sgu_w[s], rhs.astype(BF16))
        mixed = mixed + jnp.concatenate([sgu_bias[:, s * LANES:(s + 1) * LANES]] * BATCH, axis=1)
        for b in range(BATCH):
            m_slab[s, pl.ds(b, CHUNK, stride=BATCH), :] = mixed[:, b * LANES:(b + 1) * LANES]
    mixed = jnp.concatenate([m_slab[s] for s in range(n_slab)], axis=1)
    return u * mixed


def _pool(zb, halo, t0, pool_w, pool_scale):
    e = jnp.concatenate([halo, zb], axis=0)
    shifted = lambda a, steps: a[:a.shape[0] - steps * BATCH]
    s2 = e[BATCH:] + shifted(e, 1)
    s4 = s2[2 * BATCH:] + shifted(s2, 2)
    s8 = s4[4 * BATCH:] + shifted(s4, 4)
    s16 = s8[8 * BATCH:] + shifted(s8, 8)
    tail = lambda a: a[a.shape[0] - MIX_ROWS:]
    lane = lax.broadcasted_iota(jnp.int32, (MIX_ROWS, W_GRP), 1)
    group = jnp.right_shift(lane, int(math.log2(POOL_GROUP_DIM)))
    win_sum = jnp.where(group == 0, tail(s2),
                        jnp.where(group == 1, tail(s4), jnp.where(group == 2, tail(s8), tail(s16))))
    window = jnp.left_shift(2, group)
    row = lax.broadcasted_iota(jnp.int32, (MIX_ROWS, W_GRP), 0)
    step = t0 + jnp.right_shift(row, int(math.log2(BATCH)))
    count = jnp.minimum(step + 1, window).astype(F32)
    p = win_sum / count - zb
    return _dot(p.astype(BF16), pool_w) * pool_scale


def _conv(zc, halo, conv_w):
    bg, cg, xh = zc[:, :W_GRP], zc[:, W_GRP:2 * W_GRP], zc[:, 2 * W_GRP:]
    y = cg * xh
    e = jnp.concatenate([halo, y], axis=0)
    out = (conv_w[0:1] * e[:MIX_ROWS] + conv_w[1:2] * e[BATCH:BATCH + MIX_ROWS] + conv_w[2:3] * y)
    return bg * out, y[MIX_ROWS - CONV_HALO:]


def _s5(zd, a_re, a_im, b_in, c_out, d_skip, glu_w, glu_b, state_ref, scan_ref):
    scan_ref[...] = _dot(zd.astype(BF16), b_in)
    ar = jnp.broadcast_to(a_re, (BATCH, N_STATE))
    ai = jnp.broadcast_to(a_im, (BATCH, N_STATE))

    def step(t, carry):
        xr, xi = carry
        r0 = pl.multiple_of(t * BATCH, BATCH)
        bu = scan_ref[pl.ds(r0, BATCH), :]
        nr = ar * xr - ai * xi + bu[:, :N_STATE]
        ni = ar * xi + ai * xr + bu[:, N_STATE:]
        scan_ref[pl.ds(r0, BATCH), :] = jnp.concatenate([nr, ni], axis=1)
        return nr, ni

    st = state_ref[...]
    xr, xi = lax.fori_loop(0, MIX_STEPS, step, (st[:, :N_STATE], st[:, N_STATE:]), unroll=2)
    state_ref[...] = jnp.concatenate([xr, xi], axis=1)
    y = _dot(scan_ref[...].astype(BF16), c_out) + d_skip * zd
    y = jax.nn.gelu(y)
    return y * jax.nn.sigmoid(_dot(y.astype(BF16), glu_w) + glu_b)


def _mix_body(x_ref, mod_ref, g_ref, win_ref, seg_ref, sguw_ref, sgub_ref, poolw_ref, pools_ref,
              convw_ref, are_ref, aim_ref, bin_ref, cout_ref, dskip_ref, gluw_ref, glub_ref,
              mg_ref, wout_ref, o_ref,
              pool_halo, conv_halo, state_ref, scan_ref, v_slab, m_slab):
    i = pl.program_id(0)

    @pl.when(i == 0)
    def _():
        pool_halo[...] = jnp.zeros_like(pool_halo)
        conv_halo[...] = jnp.zeros_like(conv_halo)
        state_ref[...] = jnp.zeros_like(state_ref)

    x = x_ref[...]
    h = _modulate(_rmsnorm(x, g_ref[...]), mod_ref[0], mod_ref[1]).astype(BF16)
    mg = mg_ref[...]

    za = _dot(h, win_ref[:, :2 * W_GRP])
    ya = _sgu(za, seg_ref[...], sguw_ref, sgub_ref[...], v_slab, m_slab)
    na = _group_norm(ya, mg[:, :W_GRP])

    zb = _dot(h, win_ref[:, 2 * W_GRP:3 * W_GRP])
    yb = _pool(zb, pool_halo[...], i * MIX_STEPS, poolw_ref[...], pools_ref[...])
    pool_halo[...] = zb[MIX_ROWS - POOL_HALO:]
    nb = _group_norm(yb, mg[:, W_GRP:2 * W_GRP])

    zc = _dot(h, win_ref[:, 3 * W_GRP:6 * W_GRP])
    yc, new_halo = _conv(zc, conv_halo[...], convw_ref[...])
    conv_halo[...] = new_halo
    nc = _group_norm(yc, mg[:, 2 * W_GRP:3 * W_GRP])

    zd = _dot(h, win_ref[:, 6 * W_GRP:])
    yd = _s5(zd, are_ref[...], aim_ref[...], bin_ref[...], cout_ref[...], dskip_ref[...],
             gluw_ref[...], glub_ref[...], state_ref, scan_ref)
    nd = _group_norm(yd, mg[:, 3 * W_GRP:])

    y = jnp.concatenate([na, nb, nc, nd], axis=1)
    out = _dot(y, wout_ref[...])
    gate = mod_ref[2]
    o_ref[...] = x + _per_batch(out, lambda o3: o3 * gate[None])


def _mix_call(x, mod, g, p):
    const = lambda shape: pl.BlockSpec(shape, lambda i: (0,) * len(shape),
                                       pipeline_mode=pl.Buffered(1))
    rows = pl.BlockSpec((MIX_ROWS, D_MODEL), lambda i: (i, 0))
    operands = [mod, g, p["w_in"], p["seg_mean"], p["sgu_w"], p["sgu_bias"], p["pool_w"],
                p["pool_scale"], p["conv_w"], p["a_re"], p["a_im"], p["b_in"], p["c_out"],
                p["d_skip"], p["glu_w"], p["glu_b"], p["mix_g"], p["w_out"]]
    return pl.pallas_call(
        _mix_body,
        grid=(N_ROWS // MIX_ROWS,),
        in_specs=[rows] + [const(o.shape) for o in operands],
        out_specs=rows,
        out_shape=jax.ShapeDtypeStruct((N_ROWS, D_MODEL), F32),
        scratch_shapes=[
            pltpu.VMEM((POOL_HALO, W_GRP), F32),
            pltpu.VMEM((CONV_HALO, W_GRP), F32),
            pltpu.VMEM((BATCH, 2 * N_STATE), F32),
            pltpu.VMEM((MIX_ROWS, 2 * N_STATE), F32),
            pltpu.VMEM((W_GRP // LANES, MIX_ROWS, LANES), F32),
            pltpu.VMEM((W_GRP // LANES, MIX_ROWS, LANES), F32),
        ],
        input_output_aliases={0: 0},
        compiler_params=pltpu.CompilerParams(
            dimension_semantics=("arbitrary",), vmem_limit_bytes=VMEM_LIMIT),
        name="mixers",
    )(x, *operands)


def _block_diag(blocks):
    g, r, c = blocks.shape
    eye = jnp.eye(g, dtype=blocks.dtype)
    return (blocks[:, :, None, :] * eye[:, None, :, None]).reshape(g * r, g * c)


def _mixer_params(l, w_mix_in, sgu_w, sgu_b, pool_w, pool_scale, conv_w, a_re, a_im, bb_re, bb_im,
                  s5_c_re, s5_c_im, s5_d, s5_glu_w, s5_glu_b, mix_norm_g, w_mix_out):
    tril = jnp.tril(jnp.ones((CHUNK, CHUNK), F32))
    w_s = sgu_w[l] * tril
    sgu_pair = jnp.concatenate([w_s[0::2], w_s[1::2]], axis=2).astype(BF16)
    sgu_bias = jnp.repeat(sgu_b[l].T, SGU_HEAD_DIM, axis=1)
    seg_mean = _block_diag(jnp.full((SGU_HEADS, SGU_HEAD_DIM, SGU_HEAD_DIM),
                                    1.0 / SGU_HEAD_DIM, F32)).astype(BF16)
    to_blocks = lambda bb: jnp.transpose(bb.reshape(S5_GROUP_CH, S5_GROUPS, S5_STATE), (1, 0, 2))
    b_in = jnp.concatenate([_block_diag(to_blocks(bb_re[l])), _block_diag(to_blocks(bb_im[l]))],
                           axis=1).astype(BF16)
    c_t = lambda c: jnp.transpose(c, (0, 2, 1))
    c_out = jnp.concatenate([_block_diag(c_t(s5_c_re[l])), -_block_diag(c_t(s5_c_im[l]))],
                            axis=0).astype(BF16)
    row = lambda v: v.reshape(1, -1)
    return {
        "w_in": w_mix_in[l].astype(BF16), "seg_mean": seg_mean, "sgu_w": sgu_pair,
        "sgu_bias": sgu_bias, "pool_w": _block_diag(pool_w[l]).astype(BF16),
        "pool_scale": row(pool_scale[l]), "conv_w": conv_w[l], "a_re": a_re[l], "a_im": a_im[l],
        "b_in": b_in, "c_out": c_out, "d_skip": row(s5_d[l]), "glu_w": s5_glu_w[l].astype(BF16),
        "glu_b": row(s5_glu_b[l]), "mix_g": row(mix_norm_g[l]), "w_out": w_mix_out[l].astype(BF16),
    }


def kernel(x, c, ada_w, ada_b, norm1_g, ffn1_w_in, ffn1_w_out, norm2_g, w_mix_in, sgu_w, sgu_b, pool_w, pool_scale, conv_w, s5_lambda_re, s5_lambda_im, s5_b_re, s5_b_im, s5_c_re, s5_c_im, s5_d, s5_log_dt, s5_glu_w, s5_glu_b, mix_norm_g, w_mix_out, norm3_g, ffn2_w_in, ffn2_w_out, final_norm_g):
    cond = _ada_call(c, ada_w, ada_b).reshape(DEPTH, BATCH, 3, 3, D_MODEL)
    cond = jnp.transpose(cond, (0, 2, 3, 1, 4))
    a_re, a_im, bb_re, bb_im = _s5_disc_call(s5_lambda_re, s5_lambda_im, s5_log_dt, s5_b_re, s5_b_im)
    row = lambda v: v.reshape(1, -1)
    final_g = row(final_norm_g)

    xs = jnp.transpose(x, (1, 0, 2)).reshape(N_ROWS, D_MODEL)
    for l in range(DEPTH):
        xs = _ffn_call(xs, cond[l, 0], row(norm1_g[l]), ffn1_w_in[l].astype(BF16),
                       ffn1_w_out[l].astype(BF16), final_g, final_norm=False)
        p = _mixer_params(l, w_mix_in, sgu_w, sgu_b, pool_w, pool_scale, conv_w, a_re, a_im,
                          bb_re, bb_im, s5_c_re, s5_c_im, s5_d, s5_glu_w, s5_glu_b, mix_norm_g,
                          w_mix_out)
        xs = _mix_call(xs, cond[l, 1], row(norm2_g[l]), p)
        xs = _ffn_call(xs, cond[l, 2], row(norm3_g[l]), ffn2_w_in[l].astype(BF16),
                       ffn2_w_out[l].astype(BF16), final_g, final_norm=(l == DEPTH - 1))
    return jnp.transpose(xs.reshape(SEQ, BATCH, D_MODEL), (1, 0, 2))
```

```python
import functools
import math

import jax
import jax.numpy as jnp
from jax import lax
from jax.experimental import pallas as pl
from jax.experimental.pallas import tpu as pltpu

F32 = jnp.float32
BF16 = jnp.bfloat16

D_MODEL = 1024
BATCH = 8
SEQ = 2048
DEPTH = 4
W_GRP = 256
SGU_HEADS = 4
SGU_HEAD_DIM = 64
CHUNK = 128
POOL_WINDOWS = (2, 4, 8, 16)
POOL_GROUP_DIM = 64
S5_GROUP_CH = 16
S5_GROUPS = 16
S5_STATE = 64
N_STATE = S5_GROUPS * S5_STATE
P_IN = 7 * W_GRP
D_FF = 2816
N_ADA = 9
EPS = 1e-6

N_ROWS = SEQ * BATCH
SUBLANES = 8
LANES = 128
VMEM_LIMIT = 56 * 1024 * 1024

FFN_ROWS = 512
FFN_CHUNK = 256
FFN_STEPS = FFN_ROWS // BATCH
MIX_STEPS = CHUNK
MIX_ROWS = MIX_STEPS * BATCH
POOL_HALO = 16 * BATCH
CONV_HALO = 2 * BATCH
ADA_COLS = 1152


def _dot(a, b):
    return jnp.dot(a, b, preferred_element_type=F32)


def _split_bf16(x):
    hi = x.astype(BF16)
    lo = (x - hi.astype(F32)).astype(BF16)
    return hi, lo


def _rmsnorm(x, g):
    ms = jnp.mean(x * x, axis=-1, keepdims=True)
    return x * lax.rsqrt(ms + EPS) * g


def _per_batch(x, fn, batch_major=False):
    rows, c = x.shape
    if batch_major:
        return fn(x.reshape(BATCH, rows // BATCH, c), lambda p: p[:, None, :]).reshape(rows, c)
    return fn(x.reshape(rows // BATCH, BATCH, c), lambda p: p[None]).reshape(rows, c)


def _modulate(y, shift, scale, batch_major=False):
    return _per_batch(y, lambda y3, ex: y3 * ex(1.0 + scale) + ex(shift), batch_major)


def _gated_residual(x, gate, y, batch_major=False):
    return x + _per_batch(y, lambda y3, ex: y3 * ex(gate), batch_major)


def _ada_body(c_ref, w_ref, b_ref, o_ref):
    c = c_ref[...]
    c_act = c * jax.nn.sigmoid(c)
    c_hi, c_lo = _split_bf16(c_act)
    w_hi, w_lo = _split_bf16(w_ref[...])
    o_ref[...] = _dot(c_hi, w_hi) + (_dot(c_lo, w_hi) + _dot(c_hi, w_lo)) + b_ref[...]


def _ada_call(c, ada_w, ada_b):
    n_col = N_ADA * D_MODEL // ADA_COLS
    return pl.pallas_call(
        _ada_body,
        grid=(DEPTH, n_col),
        in_specs=[
            pl.BlockSpec((BATCH, D_MODEL), lambda l, j: (0, 0)),
            pl.BlockSpec((None, D_MODEL, ADA_COLS), lambda l, j: (l, 0, j)),
            pl.BlockSpec((None, 1, ADA_COLS), lambda l, j: (l, 0, j)),
        ],
        out_specs=pl.BlockSpec((None, BATCH, ADA_COLS), lambda l, j: (l, 0, j)),
        out_shape=jax.ShapeDtypeStruct((DEPTH, BATCH, N_ADA * D_MODEL), F32),
        compiler_params=pltpu.CompilerParams(
            dimension_semantics=("arbitrary", "arbitrary"), vmem_limit_bytes=VMEM_LIMIT),
        name="ada_cond",
    )(c, ada_w, ada_b.reshape(DEPTH, 1, N_ADA * D_MODEL))


def _s5_disc_body(lre_ref, lim_ref, ldt_ref, br_ref, bi_ref, are_ref, aim_ref, bbr_ref, bbi_ref):
    lre, lim = lre_ref[...], lim_ref[...]
    dt = jnp.exp(ldt_ref[...])
    mag = jnp.exp(lre * dt)
    ang = lim * dt
    a_re, a_im = mag * jnp.cos(ang), mag * jnp.sin(ang)
    nr, ni = a_re - 1.0, a_im
    den = lre * lre + lim * lim
    k_re = (nr * lre + ni * lim) / den
    k_im = (ni * lre - nr * lim) / den
    br, bi = br_ref[...], bi_ref[...]
    are_ref[...] = a_re
    aim_ref[...] = a_im
    bbr_ref[...] = k_re * br - k_im * bi
    bbi_ref[...] = k_re * bi + k_im * br


def _s5_disc_call(lam_re, lam_im, log_dt, b_re, b_im):
    row = lambda a: a.reshape(DEPTH, 1, N_STATE)
    chan_major = lambda b: jnp.transpose(b, (0, 3, 1, 2)).reshape(DEPTH, S5_GROUP_CH, N_STATE)
    ldt = jnp.repeat(log_dt, S5_STATE, axis=-1)
    vec = pl.BlockSpec((None, 1, N_STATE), lambda l: (l, 0, 0))
    mat = pl.BlockSpec((None, S5_GROUP_CH, N_STATE), lambda l: (l, 0, 0))
    return pl.pallas_call(
        _s5_disc_body,
        grid=(DEPTH,),
        in_specs=[vec, vec, vec, mat, mat],
        out_specs=[vec, vec, mat, mat],
        out_shape=[jax.ShapeDtypeStruct((DEPTH, 1, N_STATE), F32)] * 2
        + [jax.ShapeDtypeStruct((DEPTH, S5_GROUP_CH, N_STATE), F32)] * 2,
        name="s5_discretise",
    )(row(lam_re), row(lam_im), row(ldt), chan_major(b_re), chan_major(b_im))


def _ffn_body(x_ref, mod_ref, g_ref, win_ref, wout_ref, fg_ref, o_ref, *slab, rows_in, rows_out,
              final_norm):
    batch_major = rows_in == "batch"
    x = x_ref[...].reshape(FFN_ROWS, D_MODEL)
    h = _modulate(_rmsnorm(x, g_ref[...]), mod_ref[0], mod_ref[1], batch_major).astype(BF16)
    acc = None
    for j in range(D_FF // FFN_CHUNK):
        lo = j * FFN_CHUNK
        a = _dot(h, win_ref[:, lo:lo + FFN_CHUNK])
        b = _dot(h, win_ref[:, D_FF + lo:D_FF + lo + FFN_CHUNK])
        gated = (a * jax.nn.sigmoid(a) * b).astype(BF16)
        part = _dot(gated, wout_ref[lo:lo + FFN_CHUNK, :])
        acc = part if acc is None else acc + part
    y = _gated_residual(x, 0.5 * mod_ref[2], acc, batch_major)
    if final_norm:
        y = _rmsnorm(y, fg_ref[...])
    if rows_in == rows_out:
        o_ref[...] = y.reshape(o_ref.shape)
        return
    (slab,) = slab
    n_slab = D_MODEL // LANES
    lanes = lambda s: slice(s * LANES, (s + 1) * LANES)
    batch_rows = lambda b: pl.ds(b, FFN_STEPS, stride=BATCH)
    if rows_out == "time":
        for s in range(n_slab):
            for b in range(BATCH):
                slab[s, batch_rows(b), :] = y[b * FFN_STEPS:(b + 1) * FFN_STEPS, lanes(s)]
        for s in range(n_slab):
            o_ref[:, lanes(s)] = slab[s]
    else:
        for s in range(n_slab):
            slab[s] = y[:, lanes(s)]
        for b in range(BATCH):
            for s in range(n_slab):
                o_ref[b, :, lanes(s)] = slab[s, batch_rows(b), :]


def _ffn_call(x, cond, g, w_in, w_out, final_g, *, layer, sub, rows_in="time", rows_out="time",
              final_norm=False):
    const = functools.partial(pl.BlockSpec, pipeline_mode=pl.Buffered(1))
    time_rows = pl.BlockSpec((FFN_ROWS, D_MODEL), lambda i: (i, 0))
    batch_rows = pl.BlockSpec((BATCH, FFN_STEPS, D_MODEL), lambda i: (0, i, 0))
    spec = {"time": time_rows, "batch": batch_rows}
    shape = {"time": (N_ROWS, D_MODEL), "batch": (BATCH, SEQ, D_MODEL)}
    reorder = rows_in != rows_out
    return pl.pallas_call(
        functools.partial(_ffn_body, rows_in=rows_in, rows_out=rows_out, final_norm=final_norm),
        grid=(N_ROWS // FFN_ROWS,),
        in_specs=[spec[rows_in],
                  const((None, None, 3, BATCH, D_MODEL), lambda i: (layer, sub, 0, 0, 0)),
                  const((None, 1, D_MODEL), lambda i: (layer, 0, 0)),
                  const((None, D_MODEL, 2 * D_FF), lambda i: (layer, 0, 0)),
                  const((None, D_FF, D_MODEL), lambda i: (layer, 0, 0)),
                  const((1, D_MODEL), lambda i: (0, 0))],
        out_specs=spec[rows_out],
        out_shape=jax.ShapeDtypeStruct(shape[rows_out], F32),
        scratch_shapes=[pltpu.VMEM((D_MODEL // LANES, FFN_ROWS, LANES), F32)] if reorder else [],
        input_output_aliases={} if reorder else {0: 0},
        compiler_params=pltpu.CompilerParams(
            dimension_semantics=("arbitrary",), vmem_limit_bytes=VMEM_LIMIT),
        name=f"ffn_{rows_in}_to_{rows_out}" + ("_final" if final_norm else ""),
    )(x, cond, g, w_in, w_out, final_g)


def _group_norm(y, g):
    ms = jnp.mean(y * y, axis=-1, keepdims=True)
    return (y * lax.rsqrt(ms + EPS) * g).astype(BF16)


def _sgu(za, seg_mean, sgu_w, sgu_bias, v_slab, m_slab):
    z = jax.nn.gelu(za)
    u, v = z[:, :W_GRP], z[:, W_GRP:]
    seg = lambda t: sum(_dot(p, seg_mean) for p in _split_bf16(t))
    d = v - seg(v)
    vn = d * lax.rsqrt(seg(d * d) + EPS)
    n_slab = W_GRP // LANES
    for s in range(n_slab):
        v_slab[s] = vn[:, s * LANES:(s + 1) * LANES]
    lane = lax.broadcasted_iota(jnp.int32, (CHUNK, LANES), 1)
    first_head = lane < SGU_HEAD_DIM
    for s in range(n_slab):
        cols = [v_slab[s, pl.ds(b, CHUNK, stride=BATCH), :] for b in range(BATCH)]
        vt = jnp.concatenate(cols, axis=1)
        keep = jnp.concatenate([first_head] * BATCH, axis=1)
        rhs = jnp.concatenate([jnp.where(keep, vt, 0.0), jnp.where(keep, 0.0, vt)], axis=0)
        mixed = _dot(sgu_w[s], rhs.astype(BF16))
        mixed = mixed + jnp.concatenate([sgu_bias[:, s * LANES:(s + 1) * LANES]] * BATCH, axis=1)
        for b in range(BATCH):
            m_slab[s, pl.ds(b, CHUNK, stride=BATCH), :] = mixed[:, b * LANES:(b + 1) * LANES]
    mixed = jnp.concatenate([m_slab[s] for s in range(n_slab)], axis=1)
    return u * mixed


def _pool(zb, halo, t0, pool_w, pool_scale):
    e = jnp.concatenate([halo, zb], axis=0)
    shifted = lambda a, steps: a[:a.shape[0] - steps * BATCH]
    s2 = e[BATCH:] + shifted(e, 1)
    s4 = s2[2 * BATCH:] + shifted(s2, 2)
    s8 = s4[4 * BATCH:] + shifted(s4, 4)
    s16 = s8[8 * BATCH:] + shifted(s8, 8)
    tail = lambda a: a[a.shape[0] - MIX_ROWS:]
    lane = lax.broadcasted_iota(jnp.int32, (MIX_ROWS, W_GRP), 1)
    group = jnp.right_shift(lane, int(math.log2(POOL_GROUP_DIM)))
    win_sum = jnp.where(group == 0, tail(s2),
                        jnp.where(group == 1, tail(s4), jnp.where(group == 2, tail(s8), tail(s16))))
    window = jnp.left_shift(2, group)
    row = lax.broadcasted_iota(jnp.int32, (MIX_ROWS, W_GRP), 0)
    step = t0 + jnp.right_shift(row, int(math.log2(BATCH)))
    count = jnp.minimum(step + 1, window).astype(F32)
    p = win_sum / count - zb
    return _dot(p.astype(BF16), pool_w) * pool_scale


def _conv(zc, halo, conv_w):
    bg, cg, xh = zc[:, :W_GRP], zc[:, W_GRP:2 * W_GRP], zc[:, 2 * W_GRP:]
    y = cg * xh
    e = jnp.concatenate([halo, y], axis=0)
    out = (conv_w[0:1] * e[:MIX_ROWS] + conv_w[1:2] * e[BATCH:BATCH + MIX_ROWS] + conv_w[2:3] * y)
    return bg * out, y[MIX_ROWS - CONV_HALO:]


def _s5(zd, a_re, a_im, b_in, c_out, d_skip, glu_w, glu_b, state_ref, scan_ref):
    scan_ref[...] = _dot(zd.astype(BF16), b_in)
    ar = jnp.broadcast_to(a_re, (BATCH, N_STATE))
    ai = jnp.broadcast_to(a_im, (BATCH, N_STATE))

    def step(t, carry):
        xr, xi = carry
        r0 = pl.multiple_of(t * BATCH, BATCH)
        bu = scan_ref[pl.ds(r0, BATCH), :]
        nr = ar * xr - ai * xi + bu[:, :N_STATE]
        ni = ar * xi + ai * xr + bu[:, N_STATE:]
        scan_ref[pl.ds(r0, BATCH), :] = jnp.concatenate([nr, ni], axis=1)
        return nr, ni

    st = state_ref[...]
    xr, xi = lax.fori_loop(0, MIX_STEPS, step, (st[:, :N_STATE], st[:, N_STATE:]), unroll=2)
    state_ref[...] = jnp.concatenate([xr, xi], axis=1)
    y = _dot(scan_ref[...].astype(BF16), c_out) + d_skip * zd
    y = jax.nn.gelu(y)
    return y * jax.nn.sigmoid(_dot(y.astype(BF16), glu_w) + glu_b)


def _mix_body(x_ref, mod_ref, g_ref, win_ref, seg_ref, sguw_ref, sgub_ref, poolw_ref, pools_ref,
              convw_ref, are_ref, aim_ref, bin_ref, cout_ref, dskip_ref, gluw_ref, glub_ref,
              mg_ref, wout_ref, o_ref,
              pool_halo, conv_halo, state_ref, scan_ref, v_slab, m_slab):
    i = pl.program_id(0)

    @pl.when(i == 0)
    def _():
        pool_halo[...] = jnp.zeros_like(pool_halo)
        conv_halo[...] = jnp.zeros_like(conv_halo)
        state_ref[...] = jnp.zeros_like(state_ref)

    x = x_ref[...]
    h = _modulate(_rmsnorm(x, g_ref[...]), mod_ref[0], mod_ref[1]).astype(BF16)
    mg = mg_ref[...]

    za = _dot(h, win_ref[:, :2 * W_GRP])
    ya = _sgu(za, seg_ref[...], sguw_ref, sgub_ref[...], v_slab, m_slab)
    na = _group_norm(ya, mg[:, :W_GRP])

    zb = _dot(h, win_ref[:, 2 * W_GRP:3 * W_GRP])
    yb = _pool(zb, pool_halo[...], i * MIX_STEPS, poolw_ref[...], pools_ref[...])
    pool_halo[...] = zb[MIX_ROWS - POOL_HALO:]
    nb = _group_norm(yb, mg[:, W_GRP:2 * W_GRP])

    zc = _dot(h, win_ref[:, 3 * W_GRP:6 * W_GRP])
    yc, new_halo = _conv(zc, conv_halo[...], convw_ref[...])
    conv_halo[...] = new_halo
    nc = _group_norm(yc, mg[:, 2 * W_GRP:3 * W_GRP])

    zd = _dot(h, win_ref[:, 6 * W_GRP:])
    yd = _s5(zd, are_ref[...], aim_ref[...], bin_ref[...], cout_ref[...], dskip_ref[...],
             gluw_ref[...], glub_ref[...], state_ref, scan_ref)
    nd = _group_norm(yd, mg[:, 3 * W_GRP:])

    y = jnp.concatenate([na, nb, nc, nd], axis=1)
    out = _dot(y, wout_ref[...])
    o_ref[...] = _gated_residual(x, mod_ref[2], out)


def _mix_call(x, cond, g, p, *, layer):
    def layer_block(a):
        zeros = (0,) * (a.ndim - 1)
        return pl.BlockSpec((None,) + a.shape[1:], lambda i: (layer,) + zeros,
                            pipeline_mode=pl.Buffered(1))
    rows = pl.BlockSpec((MIX_ROWS, D_MODEL), lambda i: (i, 0))
    operands = [g, p["w_in"], p["seg_mean"], p["sgu_w"], p["sgu_bias"], p["pool_w"],
                p["pool_scale"], p["conv_w"], p["a_re"], p["a_im"], p["b_in"], p["c_out"],
                p["d_skip"], p["glu_w"], p["glu_b"], p["mix_g"], p["w_out"]]
    cond_spec = pl.BlockSpec((None, None, 3, BATCH, D_MODEL), lambda i: (layer, 1, 0, 0, 0),
                             pipeline_mode=pl.Buffered(1))
    return pl.pallas_call(
        _mix_body,
        grid=(N_ROWS // MIX_ROWS,),
        in_specs=[rows, cond_spec] + [layer_block(o) for o in operands],
        out_specs=rows,
        out_shape=jax.ShapeDtypeStruct((N_ROWS, D_MODEL), F32),
        scratch_shapes=[
            pltpu.VMEM((POOL_HALO, W_GRP), F32),
            pltpu.VMEM((CONV_HALO, W_GRP), F32),
            pltpu.VMEM((BATCH, 2 * N_STATE), F32),
            pltpu.VMEM((MIX_ROWS, 2 * N_STATE), F32),
            pltpu.VMEM((W_GRP // LANES, MIX_ROWS, LANES), F32),
            pltpu.VMEM((W_GRP // LANES, MIX_ROWS, LANES), F32),
        ],
        input_output_aliases={0: 0},
        compiler_params=pltpu.CompilerParams(
            dimension_semantics=("arbitrary",), vmem_limit_bytes=VMEM_LIMIT),
        name="mixers",
    )(x, cond, *operands)


def _block_diag(blocks):
    l, g, r, c = blocks.shape
    eye = jnp.eye(g, dtype=blocks.dtype)
    return (blocks[:, :, :, None, :] * eye[None, :, None, :, None]).reshape(l, g * r, g * c)


def _mixer_params(w_mix_in, sgu_w, sgu_b, pool_w, pool_scale, conv_w, a_re, a_im, bb_re, bb_im,
                  s5_c_re, s5_c_im, s5_d, s5_glu_w, s5_glu_b, mix_norm_g, w_mix_out):
    tril = jnp.tril(jnp.ones((CHUNK, CHUNK), F32))
    w_s = sgu_w * tril
    sgu_pair = jnp.concatenate([w_s[:, 0::2], w_s[:, 1::2]], axis=3).astype(BF16)
    sgu_bias = jnp.repeat(jnp.swapaxes(sgu_b, 1, 2), SGU_HEAD_DIM, axis=2)
    seg_mean = _block_diag(jnp.full((DEPTH, SGU_HEADS, SGU_HEAD_DIM, SGU_HEAD_DIM),
                                    1.0 / SGU_HEAD_DIM, F32)).astype(BF16)
    to_blocks = lambda bb: jnp.transpose(
        bb.reshape(DEPTH, S5_GROUP_CH, S5_GROUPS, S5_STATE), (0, 2, 1, 3))
    b_in = jnp.concatenate([_block_diag(to_blocks(bb_re)), _block_diag(to_blocks(bb_im))],
                           axis=2).astype(BF16)
    c_t = lambda c: jnp.swapaxes(c, 2, 3)
    c_out = jnp.concatenate([_block_diag(c_t(s5_c_re)), -_block_diag(c_t(s5_c_im))],
                            axis=1).astype(BF16)
    row = lambda v: v.reshape(DEPTH, 1, -1)
    return {
        "w_in": w_mix_in.astype(BF16), "seg_mean": seg_mean, "sgu_w": sgu_pair,
        "sgu_bias": sgu_bias, "pool_w": _block_diag(pool_w).astype(BF16),
        "pool_scale": row(pool_scale), "conv_w": conv_w, "a_re": a_re, "a_im": a_im,
        "b_in": b_in, "c_out": c_out, "d_skip": row(s5_d), "glu_w": s5_glu_w.astype(BF16),
        "glu_b": row(s5_glu_b), "mix_g": row(mix_norm_g), "w_out": w_mix_out.astype(BF16),
    }


def kernel(x, c, ada_w, ada_b, norm1_g, ffn1_w_in, ffn1_w_out, norm2_g, w_mix_in, sgu_w, sgu_b, pool_w, pool_scale, conv_w, s5_lambda_re, s5_lambda_im, s5_b_re, s5_b_im, s5_c_re, s5_c_im, s5_d, s5_log_dt, s5_glu_w, s5_glu_b, mix_norm_g, w_mix_out, norm3_g, ffn2_w_in, ffn2_w_out, final_norm_g):
    cond = _ada_call(c, ada_w, ada_b).reshape(DEPTH, BATCH, 3, 3, D_MODEL)
    cond = jnp.transpose(cond, (0, 2, 3, 1, 4))
    a_re, a_im, bb_re, bb_im = _s5_disc_call(s5_lambda_re, s5_lambda_im, s5_log_dt, s5_b_re, s5_b_im)
    p = _mixer_params(w_mix_in, sgu_w, sgu_b, pool_w, pool_scale, conv_w, a_re, a_im, bb_re, bb_im,
                      s5_c_re, s5_c_im, s5_d, s5_glu_w, s5_glu_b, mix_norm_g, w_mix_out)
    row = lambda v: v.reshape(DEPTH, 1, D_MODEL)
    g1, g2, g3 = row(norm1_g), row(norm2_g), row(norm3_g)
    final_g = final_norm_g.reshape(1, D_MODEL)
    w1_in, w1_out = ffn1_w_in.astype(BF16), ffn1_w_out.astype(BF16)
    w2_in, w2_out = ffn2_w_in.astype(BF16), ffn2_w_out.astype(BF16)

    xs = x
    for l in range(DEPTH):
        last = l == DEPTH - 1
        xs = _ffn_call(xs, cond, g1, w1_in, w1_out, final_g, layer=l, sub=0,
                       rows_in="batch" if l == 0 else "time")
        xs = _mix_call(xs, cond, g2, p, layer=l)
        xs = _ffn_call(xs, cond, g3, w2_in, w2_out, final_g, layer=l, sub=2,
                       rows_out="batch" if last else "time", final_norm=last)
    return xs
```

```python
import functools
import math

import jax
import jax.numpy as jnp
from jax import lax
from jax.experimental import pallas as pl
from jax.experimental.pallas import tpu as pltpu

F32 = jnp.float32
BF16 = jnp.bfloat16

D_MODEL = 1024
BATCH = 8
SEQ = 2048
DEPTH = 4
W_GRP = 256
SGU_HEADS = 4
SGU_HEAD_DIM = 64
CHUNK = 128
POOL_WINDOWS = (2, 4, 8, 16)
POOL_GROUP_DIM = 64
S5_GROUP_CH = 16
S5_GROUPS = 16
S5_STATE = 64
N_STATE = S5_GROUPS * S5_STATE
P_IN = 7 * W_GRP
D_FF = 2816
N_ADA = 9
EPS = 1e-6

N_ROWS = SEQ * BATCH
SUBLANES = 8
LANES = 128
VMEM_LIMIT = 56 * 1024 * 1024

FFN_ROWS = 512
FFN_CHUNK = 256
FFN_STEPS = FFN_ROWS // BATCH
MIX_STEPS = CHUNK // 2
MIX_ROWS = MIX_STEPS * BATCH
POOL_HALO = 16 * BATCH
CONV_HALO = 2 * BATCH
ADA_COLS = 1152


def _dot(a, b):
    return jnp.dot(a, b, preferred_element_type=F32)


def _split_bf16(x):
    hi = x.astype(BF16)
    lo = (x - hi.astype(F32)).astype(BF16)
    return hi, lo


def _rmsnorm(x, g):
    ms = jnp.mean(x * x, axis=-1, keepdims=True)
    return x * lax.rsqrt(ms + EPS) * g


def _per_batch(x, fn, batch_major=False):
    rows, c = x.shape
    if batch_major:
        return fn(x.reshape(BATCH, rows // BATCH, c), lambda p: p[:, None, :]).reshape(rows, c)
    return fn(x.reshape(rows // BATCH, BATCH, c), lambda p: p[None]).reshape(rows, c)


def _modulate(y, shift, scale, batch_major=False):
    return _per_batch(y, lambda y3, ex: y3 * ex(1.0 + scale) + ex(shift), batch_major)


def _gated_residual(x, gate, y, batch_major=False):
    return x + _per_batch(y, lambda y3, ex: y3 * ex(gate), batch_major)


def _ada_body(c_ref, w_ref, b_ref, o_ref):
    c = c_ref[...]
    c_act = c * jax.nn.sigmoid(c)
    c_hi, c_lo = _split_bf16(c_act)
    w_hi, w_lo = _split_bf16(w_ref[...])
    o_ref[...] = _dot(c_hi, w_hi) + (_dot(c_lo, w_hi) + _dot(c_hi, w_lo)) + b_ref[...]


def _ada_call(c, ada_w, ada_b):
    n_col = N_ADA * D_MODEL // ADA_COLS
    return pl.pallas_call(
        _ada_body,
        grid=(DEPTH, n_col),
        in_specs=[
            pl.BlockSpec((BATCH, D_MODEL), lambda l, j: (0, 0)),
            pl.BlockSpec((None, D_MODEL, ADA_COLS), lambda l, j: (l, 0, j)),
            pl.BlockSpec((None, 1, ADA_COLS), lambda l, j: (l, 0, j)),
        ],
        out_specs=pl.BlockSpec((None, BATCH, ADA_COLS), lambda l, j: (l, 0, j)),
        out_shape=jax.ShapeDtypeStruct((DEPTH, BATCH, N_ADA * D_MODEL), F32),
        compiler_params=pltpu.CompilerParams(
            dimension_semantics=("arbitrary", "arbitrary"), vmem_limit_bytes=VMEM_LIMIT),
        name="ada_cond",
    )(c, ada_w, ada_b.reshape(DEPTH, 1, N_ADA * D_MODEL))


def _s5_disc_body(lre_ref, lim_ref, ldt_ref, br_ref, bi_ref, are_ref, aim_ref, bbr_ref, bbi_ref):
    lre, lim = lre_ref[...], lim_ref[...]
    dt = jnp.exp(ldt_ref[...])
    mag = jnp.exp(lre * dt)
    ang = lim * dt
    a_re, a_im = mag * jnp.cos(ang), mag * jnp.sin(ang)
    nr, ni = a_re - 1.0, a_im
    den = lre * lre + lim * lim
    k_re = (nr * lre + ni * lim) / den
    k_im = (ni * lre - nr * lim) / den
    br, bi = br_ref[...], bi_ref[...]
    are_ref[...] = a_re
    aim_ref[...] = a_im
    bbr_ref[...] = k_re * br - k_im * bi
    bbi_ref[...] = k_re * bi + k_im * br


def _s5_disc_call(lam_re, lam_im, log_dt, b_re, b_im):
    row = lambda a: a.reshape(DEPTH, 1, N_STATE)
    chan_major = lambda b: jnp.transpose(b, (0, 3, 1, 2)).reshape(DEPTH, S5_GROUP_CH, N_STATE)
    ldt = jnp.repeat(log_dt, S5_STATE, axis=-1)
    vec = pl.BlockSpec((None, 1, N_STATE), lambda l: (l, 0, 0))
    mat = pl.BlockSpec((None, S5_GROUP_CH, N_STATE), lambda l: (l, 0, 0))
    return pl.pallas_call(
        _s5_disc_body,
        grid=(DEPTH,),
        in_specs=[vec, vec, vec, mat, mat],
        out_specs=[vec, vec, mat, mat],
        out_shape=[jax.ShapeDtypeStruct((DEPTH, 1, N_STATE), F32)] * 2
        + [jax.ShapeDtypeStruct((DEPTH, S5_GROUP_CH, N_STATE), F32)] * 2,
        name="s5_discretise",
    )(row(lam_re), row(lam_im), row(ldt), chan_major(b_re), chan_major(b_im))


def _ffn_body(x_ref, mod_ref, g_ref, win_ref, wout_ref, fg_ref, o_ref, *slab, rows_in, rows_out,
              final_norm):
    batch_major = rows_in == "batch"
    x = x_ref[...].reshape(FFN_ROWS, D_MODEL)
    h = _modulate(_rmsnorm(x, g_ref[...]), mod_ref[0], mod_ref[1], batch_major).astype(BF16)
    acc = None
    for j in range(D_FF // FFN_CHUNK):
        lo = j * FFN_CHUNK
        a = _dot(h, win_ref[:, lo:lo + FFN_CHUNK])
        b = _dot(h, win_ref[:, D_FF + lo:D_FF + lo + FFN_CHUNK])
        gated = (a * jax.nn.sigmoid(a) * b).astype(BF16)
        part = _dot(gated, wout_ref[lo:lo + FFN_CHUNK, :])
        acc = part if acc is None else acc + part
    y = _gated_residual(x, 0.5 * mod_ref[2], acc, batch_major)
    if final_norm:
        y = _rmsnorm(y, fg_ref[...])
    if rows_in == rows_out:
        o_ref[...] = y.reshape(o_ref.shape)
        return
    (slab,) = slab
    n_slab = D_MODEL // LANES
    lanes = lambda s: slice(s * LANES, (s + 1) * LANES)
    batch_rows = lambda b: pl.ds(b, FFN_STEPS, stride=BATCH)
    if rows_out == "time":
        for s in range(n_slab):
            for b in range(BATCH):
                slab[s, batch_rows(b), :] = y[b * FFN_STEPS:(b + 1) * FFN_STEPS, lanes(s)]
        for s in range(n_slab):
            o_ref[:, lanes(s)] = slab[s]
    else:
        for s in range(n_slab):
            slab[s] = y[:, lanes(s)]
        for b in range(BATCH):
            for s in range(n_slab):
                o_ref[b, :, lanes(s)] = slab[s, batch_rows(b), :]


def _ffn_call(x, cond, g, w_in, w_out, final_g, *, layer, sub, rows_in="time", rows_out="time",
              final_norm=False):
    const = functools.partial(pl.BlockSpec, pipeline_mode=pl.Buffered(1))
    time_rows = pl.BlockSpec((FFN_ROWS, D_MODEL), lambda i: (i, 0))
    batch_rows = pl.BlockSpec((BATCH, FFN_STEPS, D_MODEL), lambda i: (0, i, 0))
    spec = {"time": time_rows, "batch": batch_rows}
    shape = {"time": (N_ROWS, D_MODEL), "batch": (BATCH, SEQ, D_MODEL)}
    reorder = rows_in != rows_out
    return pl.pallas_call(
        functools.partial(_ffn_body, rows_in=rows_in, rows_out=rows_out, final_norm=final_norm),
        grid=(N_ROWS // FFN_ROWS,),
        in_specs=[spec[rows_in],
                  const((None, None, 3, BATCH, D_MODEL), lambda i: (layer, sub, 0, 0, 0)),
                  const((None, 1, D_MODEL), lambda i: (layer, 0, 0)),
                  const((None, D_MODEL, 2 * D_FF), lambda i: (layer, 0, 0)),
                  const((None, D_FF, D_MODEL), lambda i: (layer, 0, 0)),
                  const((1, D_MODEL), lambda i: (0, 0))],
        out_specs=spec[rows_out],
        out_shape=jax.ShapeDtypeStruct(shape[rows_out], F32),
        scratch_shapes=[pltpu.VMEM((D_MODEL // LANES, FFN_ROWS, LANES), F32)] if reorder else [],
        compiler_params=pltpu.CompilerParams(
            dimension_semantics=("arbitrary",), vmem_limit_bytes=VMEM_LIMIT),
        name=f"ffn_{rows_in}_to_{rows_out}" + ("_final" if final_norm else ""),
    )(x, cond, g, w_in, w_out, final_g)


def _group_norm(y, g):
    ms = jnp.mean(y * y, axis=-1, keepdims=True)
    return (y * lax.rsqrt(ms + EPS) * g).astype(BF16)


def _sgu_norm(za, seg_mean, v_hist):
    z = jax.nn.gelu(za)
    u, v = z[:, :W_GRP], z[:, W_GRP:]
    seg = lambda t: _dot(t.astype(BF16), seg_mean)
    d = v - seg(v)
    vn = d * lax.rsqrt(seg(d * d) + EPS)
    for s in range(W_GRP // LANES):
        v_hist[s, MIX_ROWS:, :] = vn[:, s * LANES:(s + 1) * LANES]
    return u


def _sgu_mix(sgu_w, sgu_bias, v_hist, m_slab):
    n_slab = W_GRP // LANES
    lane = lax.broadcasted_iota(jnp.int32, (CHUNK, BATCH * LANES), 1)
    first_head = jnp.bitwise_and(lane, LANES - 1) < SGU_HEAD_DIM
    for s in range(n_slab):
        cols = [v_hist[s, pl.ds(b, CHUNK, stride=BATCH), :] for b in range(BATCH)]
        vt = jnp.concatenate(cols, axis=1)
        rhs = jnp.concatenate([jnp.where(first_head, vt, 0.0), jnp.where(first_head, 0.0, vt)],
                              axis=0)
        mixed = _dot(sgu_w[s], rhs.astype(BF16))
        mixed = mixed + jnp.concatenate([sgu_bias[:, s * LANES:(s + 1) * LANES]] * BATCH, axis=1)
        for b in range(BATCH):
            m_slab[s, pl.ds(b, MIX_STEPS, stride=BATCH), :] = mixed[:, b * LANES:(b + 1) * LANES]
        v_hist[s, :MIX_ROWS, :] = v_hist[s, MIX_ROWS:, :]
    return jnp.concatenate([m_slab[s] for s in range(n_slab)], axis=1)


def _pool(zb, halo, t0, pool_w, pool_scale):
    e = jnp.concatenate([halo, zb], axis=0)
    shifted = lambda a, steps: a[:a.shape[0] - steps * BATCH]
    s2 = e[BATCH:] + shifted(e, 1)
    s4 = s2[2 * BATCH:] + shifted(s2, 2)
    s8 = s4[4 * BATCH:] + shifted(s4, 4)
    s16 = s8[8 * BATCH:] + shifted(s8, 8)
    tail = lambda a: a[a.shape[0] - MIX_ROWS:]
    lane = lax.broadcasted_iota(jnp.int32, (MIX_ROWS, W_GRP), 1)
    group = jnp.right_shift(lane, int(math.log2(POOL_GROUP_DIM)))
    win_sum = jnp.where(group == 0, tail(s2),
                        jnp.where(group == 1, tail(s4), jnp.where(group == 2, tail(s8), tail(s16))))
    window = jnp.left_shift(2, group)
    row = lax.broadcasted_iota(jnp.int32, (MIX_ROWS, W_GRP), 0)
    step = t0 + jnp.right_shift(row, int(math.log2(BATCH)))
    count = jnp.clip(step + 1, 1, window).astype(F32)
    p = win_sum / count - zb
    return _dot(p.astype(BF16), pool_w) * pool_scale


def _conv(zc, halo, conv_w):
    bg, cg, xh = zc[:, :W_GRP], zc[:, W_GRP:2 * W_GRP], zc[:, 2 * W_GRP:]
    y = cg * xh
    e = jnp.concatenate([halo, y], axis=0)
    out = (conv_w[0:1] * e[:MIX_ROWS] + conv_w[1:2] * e[BATCH:BATCH + MIX_ROWS] + conv_w[2:3] * y)
    return bg * out, y[MIX_ROWS - CONV_HALO:]


def _s5_scan(a_re, a_im, state_ref, scan_ref, t_lo, t_hi):
    ar = jnp.broadcast_to(a_re, (BATCH, N_STATE))
    ai = jnp.broadcast_to(a_im, (BATCH, N_STATE))
    st = state_ref[...]
    xr, xi = st[:, :N_STATE], st[:, N_STATE:]
    for t in range(t_lo, t_hi):
        rows = slice(t * BATCH, (t + 1) * BATCH)
        bu = scan_ref[rows, :]
        xr, xi = (ar * xr - ai * xi + bu[:, :N_STATE], ar * xi + ai * xr + bu[:, N_STATE:])
        scan_ref[rows, :] = jnp.concatenate([xr, xi], axis=1)
    state_ref[...] = jnp.concatenate([xr, xi], axis=1)


def _s5_out(zd, states, c_out, d_skip, glu_w, glu_b):
    y = _dot(states.astype(BF16), c_out) + d_skip * zd
    y = jax.nn.gelu(y)
    return y * jax.nn.sigmoid(_dot(y.astype(BF16), glu_w) + glu_b)


def _mix_body(x_ref, mod_ref, g_ref, win_ref, seg_ref, sguw_ref, sgub_ref, poolw_ref, pools_ref,
              convw_ref, are_ref, aim_ref, bin_ref, cout_ref, dskip_ref, gluw_ref, glub_ref,
              mg_ref, wout_ref, o_ref,
              z_even, z_odd, x_even, x_odd, pool_halo, conv_halo, state_ref, scan_ref, v_hist,
              m_slab):
    i = pl.program_id(0)

    @pl.when(i == 0)
    def _():
        for ref in (z_odd, x_odd, pool_halo, conv_halo, state_ref, v_hist):
            ref[...] = jnp.zeros_like(ref)

    def step(z_a, x_a, z_b, x_b, parity_b):
        x = x_ref[...]
        h = _modulate(_rmsnorm(x, g_ref[...]), mod_ref[0], mod_ref[1]).astype(BF16)
        x_a[...] = x

        def project(k):
            cols = slice(k * W_GRP, (k + 1) * W_GRP)
            z_a[:, cols] = _dot(h, win_ref[:, cols])

        mg = mg_ref[...]
        a_re, a_im = are_ref[...], aim_ref[...]
        zd = z_b[:, 6 * W_GRP:]
        scan_ref[...] = _dot(zd.astype(BF16), bin_ref[...])
        project(0)
        u = _sgu_norm(z_b[:, :2 * W_GRP], seg_ref[...], v_hist)
        project(1)
        ya = u * _sgu_mix(sguw_ref[parity_b], sgub_ref[parity_b], v_hist, m_slab)
        na = _group_norm(ya, mg[:, :W_GRP])
        project(2)
        _s5_scan(a_re, a_im, state_ref, scan_ref, 0, MIX_STEPS // 2)
        project(3)
        zb = z_b[:, 2 * W_GRP:3 * W_GRP]
        yb = _pool(zb, pool_halo[...], (i - 1) * MIX_STEPS, poolw_ref[...], pools_ref[...])
        pool_halo[...] = zb[MIX_ROWS - POOL_HALO:]
        nb = _group_norm(yb, mg[:, W_GRP:2 * W_GRP])
        project(4)
        _s5_scan(a_re, a_im, state_ref, scan_ref, MIX_STEPS // 2, MIX_STEPS)
        project(5)
        yc, new_halo = _conv(z_b[:, 3 * W_GRP:6 * W_GRP], conv_halo[...], convw_ref[...])
        conv_halo[...] = new_halo
        nc = _group_norm(yc, mg[:, 2 * W_GRP:3 * W_GRP])
        project(6)
        yd = _s5_out(zd, scan_ref[...], cout_ref[...], dskip_ref[...], gluw_ref[...],
                     glub_ref[...])
        nd = _group_norm(yd, mg[:, 3 * W_GRP:])

        y = jnp.concatenate([na, nb, nc, nd], axis=1)
        out = _dot(y, wout_ref[...])
        o_ref[...] = _gated_residual(x_b[...], mod_ref[2], out)

    @pl.when(i % 2 == 0)
    def _():
        step(z_even, x_even, z_odd, x_odd, parity_b=1)

    @pl.when(i % 2 == 1)
    def _():
        step(z_odd, x_odd, z_even, x_even, parity_b=0)


def _mix_call(x, cond, g, p, *, layer):
    def layer_block(a):
        zeros = (0,) * (a.ndim - 1)
        return pl.BlockSpec((None,) + a.shape[1:], lambda i: (layer,) + zeros,
                            pipeline_mode=pl.Buffered(1))
    n_tiles = N_ROWS // MIX_ROWS
    operands = [g, p["w_in"], p["seg_mean"], p["sgu_w"], p["sgu_bias"], p["pool_w"],
                p["pool_scale"], p["conv_w"], p["a_re"], p["a_im"], p["b_in"], p["c_out"],
                p["d_skip"], p["glu_w"], p["glu_b"], p["mix_g"], p["w_out"]]
    cond_spec = pl.BlockSpec((None, None, 3, BATCH, D_MODEL), lambda i: (layer, 1, 0, 0, 0),
                             pipeline_mode=pl.Buffered(1))
    return pl.pallas_call(
        _mix_body,
        grid=(n_tiles + 1,),
        in_specs=[pl.BlockSpec((MIX_ROWS, D_MODEL), lambda i: (jnp.minimum(i, n_tiles - 1), 0)),
                  cond_spec] + [layer_block(o) for o in operands],
        out_specs=pl.BlockSpec((MIX_ROWS, D_MODEL), lambda i: (jnp.maximum(i - 1, 0), 0)),
        out_shape=jax.ShapeDtypeStruct((N_ROWS, D_MODEL), F32),
        scratch_shapes=[
            pltpu.VMEM((MIX_ROWS, P_IN), F32),
            pltpu.VMEM((MIX_ROWS, P_IN), F32),
            pltpu.VMEM((MIX_ROWS, D_MODEL), F32),
            pltpu.VMEM((MIX_ROWS, D_MODEL), F32),
            pltpu.VMEM((POOL_HALO, W_GRP), F32),
            pltpu.VMEM((CONV_HALO, W_GRP), F32),
            pltpu.VMEM((BATCH, 2 * N_STATE), F32),
            pltpu.VMEM((MIX_ROWS, 2 * N_STATE), F32),
            pltpu.VMEM((W_GRP // LANES, 2 * MIX_ROWS, LANES), F32),
            pltpu.VMEM((W_GRP // LANES, MIX_ROWS, LANES), F32),
        ],
        compiler_params=pltpu.CompilerParams(
            dimension_semantics=("arbitrary",), vmem_limit_bytes=VMEM_LIMIT),
        name="mixers",
    )(x, cond, *operands)


def _block_diag(blocks):
    l, g, r, c = blocks.shape
    eye = jnp.eye(g, dtype=blocks.dtype)
    return (blocks[:, :, :, None, :] * eye[None, :, None, :, None]).reshape(l, g * r, g * c)


def _mixer_params(w_mix_in, sgu_w, sgu_b, pool_w, pool_scale, conv_w, a_re, a_im, bb_re, bb_im,
                  s5_c_re, s5_c_im, s5_d, s5_glu_w, s5_glu_b, mix_norm_g, w_mix_out):
    tril = jnp.tril(jnp.ones((CHUNK, CHUNK), F32))
    w_s = sgu_w * tril
    first = jnp.concatenate([jnp.zeros_like(w_s[:, :, :MIX_STEPS, :MIX_STEPS]),
                             w_s[:, :, :MIX_STEPS, :MIX_STEPS]], axis=3)
    w_half = jnp.stack([first, w_s[:, :, MIX_STEPS:, :]], axis=1)
    sgu_pair = jnp.concatenate([w_half[:, :, 0::2], w_half[:, :, 1::2]], axis=4).astype(BF16)
    sgu_bias = jnp.repeat(jnp.swapaxes(sgu_b, 1, 2), SGU_HEAD_DIM, axis=2)
    sgu_bias = sgu_bias.reshape(DEPTH, 2, MIX_STEPS, W_GRP)
    seg_mean = _block_diag(jnp.full((DEPTH, SGU_HEADS, SGU_HEAD_DIM, SGU_HEAD_DIM),
                                    1.0 / SGU_HEAD_DIM, F32)).astype(BF16)
    to_blocks = lambda bb: jnp.transpose(
        bb.reshape(DEPTH, S5_GROUP_CH, S5_GROUPS, S5_STATE), (0, 2, 1, 3))
    b_in = jnp.concatenate([_block_diag(to_blocks(bb_re)), _block_diag(to_blocks(bb_im))],
                           axis=2).astype(BF16)
    c_t = lambda c: jnp.swapaxes(c, 2, 3)
    c_out = jnp.concatenate([_block_diag(c_t(s5_c_re)), -_block_diag(c_t(s5_c_im))],
                            axis=1).astype(BF16)
    row = lambda v: v.reshape(DEPTH, 1, -1)
    return {
        "w_in": w_mix_in.astype(BF16), "seg_mean": seg_mean, "sgu_w": sgu_pair,
        "sgu_bias": sgu_bias, "pool_w": _block_diag(pool_w).astype(BF16),
        "pool_scale": row(pool_scale), "conv_w": conv_w, "a_re": a_re, "a_im": a_im,
        "b_in": b_in, "c_out": c_out, "d_skip": row(s5_d), "glu_w": s5_glu_w.astype(BF16),
        "glu_b": row(s5_glu_b), "mix_g": row(mix_norm_g), "w_out": w_mix_out.astype(BF16),
    }


def kernel(x, c, ada_w, ada_b, norm1_g, ffn1_w_in, ffn1_w_out, norm2_g, w_mix_in, sgu_w, sgu_b, pool_w, pool_scale, conv_w, s5_lambda_re, s5_lambda_im, s5_b_re, s5_b_im, s5_c_re, s5_c_im, s5_d, s5_log_dt, s5_glu_w, s5_glu_b, mix_norm_g, w_mix_out, norm3_g, ffn2_w_in, ffn2_w_out, final_norm_g):
    cond = _ada_call(c, ada_w, ada_b).reshape(DEPTH, BATCH, 3, 3, D_MODEL)
    cond = jnp.transpose(cond, (0, 2, 3, 1, 4))
    a_re, a_im, bb_re, bb_im = _s5_disc_call(s5_lambda_re, s5_lambda_im, s5_log_dt, s5_b_re, s5_b_im)
    p = _mixer_params(w_mix_in, sgu_w, sgu_b, pool_w, pool_scale, conv_w, a_re, a_im, bb_re, bb_im,
                      s5_c_re, s5_c_im, s5_d, s5_glu_w, s5_glu_b, mix_norm_g, w_mix_out)
    row = lambda v: v.reshape(DEPTH, 1, D_MODEL)
    g1, g2, g3 = row(norm1_g), row(norm2_g), row(norm3_g)
    final_g = final_norm_g.reshape(1, D_MODEL)
    w1_in, w1_out = ffn1_w_in.astype(BF16), ffn1_w_out.astype(BF16)
    w2_in, w2_out = ffn2_w_in.astype(BF16), ffn2_w_out.astype(BF16)

    xs = x
    for l in range(DEPTH):
        last = l == DEPTH - 1
        xs = _ffn_call(xs, cond, g1, w1_in, w1_out, final_g, layer=l, sub=0,
                       rows_in="batch" if l == 0 else "time")
        xs = _mix_call(xs, cond, g2, p, layer=l)
        xs = _ffn_call(xs, cond, g3, w2_in, w2_out, final_g, layer=l, sub=2,
                       rows_out="batch" if last else "time", final_norm=last)
    return xs
```

```python
import functools
import math

import jax
import jax.numpy as jnp
from jax import lax
from jax.experimental import pallas as pl
from jax.experimental.pallas import tpu as pltpu

F32 = jnp.float32
BF16 = jnp.bfloat16

D_MODEL = 1024
BATCH = 8
SEQ = 2048
DEPTH = 4
W_GRP = 256
SGU_HEADS = 4
SGU_HEAD_DIM = 64
CHUNK = 128
POOL_WINDOWS = (2, 4, 8, 16)
POOL_GROUP_DIM = 64
S5_GROUP_CH = 16
S5_GROUPS = 16
S5_STATE = 64
N_STATE = S5_GROUPS * S5_STATE
P_IN = 7 * W_GRP
D_FF = 2816
N_ADA = 9
EPS = 1e-6

N_ROWS = SEQ * BATCH
SUBLANES = 8
LANES = 128
VMEM_LIMIT = 56 * 1024 * 1024

FFN_ROWS = 512
FFN_SUB = 2
FFN_CHUNK = 256
FFN_STEPS = FFN_ROWS // BATCH
MIX_STEPS = CHUNK // 2
MIX_ROWS = MIX_STEPS * BATCH
POOL_HALO = 16 * BATCH
CONV_HALO = 2 * BATCH
ADA_COLS = 1152


def _dot(a, b):
    return jnp.dot(a, b, preferred_element_type=F32)


def _split_bf16(x):
    hi = x.astype(BF16)
    lo = (x - hi.astype(F32)).astype(BF16)
    return hi, lo


def _rmsnorm(x, g):
    ms = jnp.mean(x * x, axis=-1, keepdims=True)
    return x * lax.rsqrt(ms + EPS) * g


def _per_batch(x, fn, batch_major=False):
    rows, c = x.shape
    if batch_major:
        return fn(x.reshape(BATCH, rows // BATCH, c), lambda p: p[:, None, :]).reshape(rows, c)
    return fn(x.reshape(rows // BATCH, BATCH, c), lambda p: p[None]).reshape(rows, c)


def _modulate(y, shift, scale, batch_major=False):
    return _per_batch(y, lambda y3, ex: y3 * ex(1.0 + scale) + ex(shift), batch_major)


def _gated_residual(x, gate, y, batch_major=False):
    return x + _per_batch(y, lambda y3, ex: y3 * ex(gate), batch_major)


def _ada_body(c_ref, w_ref, b_ref, o_ref):
    c = c_ref[...]
    c_act = c * jax.nn.sigmoid(c)
    c_hi, c_lo = _split_bf16(c_act)
    w_hi, w_lo = _split_bf16(w_ref[...])
    o_ref[...] = _dot(c_hi, w_hi) + (_dot(c_lo, w_hi) + _dot(c_hi, w_lo)) + b_ref[...]


def _ada_call(c, ada_w, ada_b):
    n_col = N_ADA * D_MODEL // ADA_COLS
    return pl.pallas_call(
        _ada_body,
        grid=(DEPTH, n_col),
        in_specs=[
            pl.BlockSpec((BATCH, D_MODEL), lambda l, j: (0, 0)),
            pl.BlockSpec((None, D_MODEL, ADA_COLS), lambda l, j: (l, 0, j)),
            pl.BlockSpec((None, 1, ADA_COLS), lambda l, j: (l, 0, j)),
        ],
        out_specs=pl.BlockSpec((None, BATCH, ADA_COLS), lambda l, j: (l, 0, j)),
        out_shape=jax.ShapeDtypeStruct((DEPTH, BATCH, N_ADA * D_MODEL), F32),
        compiler_params=pltpu.CompilerParams(
            dimension_semantics=("arbitrary", "arbitrary"), vmem_limit_bytes=VMEM_LIMIT),
        name="ada_cond",
    )(c, ada_w, ada_b.reshape(DEPTH, 1, N_ADA * D_MODEL))


def _s5_disc_body(lre_ref, lim_ref, ldt_ref, br_ref, bi_ref, are_ref, aim_ref, bbr_ref, bbi_ref):
    lre, lim = lre_ref[...], lim_ref[...]
    dt = jnp.exp(ldt_ref[...])
    mag = jnp.exp(lre * dt)
    ang = lim * dt
    a_re, a_im = mag * jnp.cos(ang), mag * jnp.sin(ang)
    nr, ni = a_re - 1.0, a_im
    den = lre * lre + lim * lim
    k_re = (nr * lre + ni * lim) / den
    k_im = (ni * lre - nr * lim) / den
    br, bi = br_ref[...], bi_ref[...]
    are_ref[...] = a_re
    aim_ref[...] = a_im
    bbr_ref[...] = k_re * br - k_im * bi
    bbi_ref[...] = k_re * bi + k_im * br


def _s5_disc_call(lam_re, lam_im, log_dt, b_re, b_im):
    row = lambda a: a.reshape(DEPTH, 1, N_STATE)
    chan_major = lambda b: jnp.transpose(b, (0, 3, 1, 2)).reshape(DEPTH, S5_GROUP_CH, N_STATE)
    ldt = jnp.repeat(log_dt, S5_STATE, axis=-1)
    vec = pl.BlockSpec((None, 1, N_STATE), lambda l: (l, 0, 0))
    mat = pl.BlockSpec((None, S5_GROUP_CH, N_STATE), lambda l: (l, 0, 0))
    return pl.pallas_call(
        _s5_disc_body,
        grid=(DEPTH,),
        in_specs=[vec, vec, vec, mat, mat],
        out_specs=[vec, vec, mat, mat],
        out_shape=[jax.ShapeDtypeStruct((DEPTH, 1, N_STATE), F32)] * 2
        + [jax.ShapeDtypeStruct((DEPTH, S5_GROUP_CH, N_STATE), F32)] * 2,
        name="s5_discretise",
    )(row(lam_re), row(lam_im), row(ldt), chan_major(b_re), chan_major(b_im))


def _ffn_body(x_ref, xn_ref, mod_ref, g_ref, win_ref, wout_ref, fg_ref, o_ref, h_ref, *slab,
              rows_in, rows_out, final_norm):
    i = pl.program_id(0)
    batch_major = rows_in == "batch"

    def load(ref, k):
        if batch_major:
            return ref[:, k * FFN_STEPS:(k + 1) * FFN_STEPS, :].reshape(FFN_ROWS, D_MODEL)
        return ref[k * FFN_ROWS:(k + 1) * FFN_ROWS, :]

    def activations(x):
        y = _rmsnorm(x, g_ref[...])
        return _modulate(y, mod_ref[0], mod_ref[1], batch_major).astype(BF16)

    def chunk(slot, j, acc):
        h = h_ref[slot]
        lo = j * FFN_CHUNK
        a = _dot(h, win_ref[:, lo:lo + FFN_CHUNK])
        b = _dot(h, win_ref[:, D_FF + lo:D_FF + lo + FFN_CHUNK])
        gated = (a * jax.nn.sigmoid(a) * b).astype(BF16)
        part = _dot(gated, wout_ref[lo:lo + FFN_CHUNK, :])
        return part if acc is None else acc + part

    def finish(k, x, acc):
        y = _gated_residual(x, 0.5 * mod_ref[2], acc, batch_major)
        if final_norm:
            y = _rmsnorm(y, fg_ref[...])
        lanes = lambda s: slice(s * LANES, (s + 1) * LANES)
        batch_rows = lambda b: pl.ds(b, FFN_STEPS, stride=BATCH)
        n_slab = D_MODEL // LANES
        if rows_in == rows_out:
            o_ref[k * FFN_ROWS:(k + 1) * FFN_ROWS, :] = y
        elif rows_out == "time":
            for s in range(n_slab):
                for b in range(BATCH):
                    slab[0][s, batch_rows(b), :] = y[b * FFN_STEPS:(b + 1) * FFN_STEPS, lanes(s)]
            for s in range(n_slab):
                o_ref[k * FFN_ROWS:(k + 1) * FFN_ROWS, lanes(s)] = slab[0][s]
        else:
            for s in range(n_slab):
                slab[0][s] = y[:, lanes(s)]
            for b in range(BATCH):
                for s in range(n_slab):
                    o_ref[b, k * FFN_STEPS:(k + 1) * FFN_STEPS, lanes(s)] = (
                        slab[0][s, batch_rows(b), :])

    cur = i % 2

    @pl.when(i == 0)
    def _():
        h_ref[0] = activations(load(x_ref, 0))

    n_chunk = D_FF // FFN_CHUNK
    x0, x1 = load(x_ref, 0), load(x_ref, 1)
    acc0 = acc1 = None
    for j in range(n_chunk):
        acc0 = chunk(cur, j, acc0)
        if j == 0:
            h_ref[2] = activations(x1)
    for j in range(n_chunk):
        acc1 = chunk(2, j, acc1)
        if j == 0:
            finish(0, x0, acc0)
        if j == 2:
            h_ref[1 - cur] = activations(load(xn_ref, 0))
    finish(1, x1, acc1)


def _ffn_call(x, cond, g, w_in, w_out, final_g, *, layer, sub, rows_in="time", rows_out="time",
              final_norm=False):
    const = functools.partial(pl.BlockSpec, pipeline_mode=pl.Buffered(1))
    n_steps = N_ROWS // (FFN_SUB * FFN_ROWS)
    n_tiles = N_ROWS // FFN_ROWS
    next_head = lambda i: jnp.minimum(FFN_SUB * (i + 1), n_tiles - 1)
    block = {"time": pl.BlockSpec((FFN_SUB * FFN_ROWS, D_MODEL), lambda i: (i, 0)),
             "batch": pl.BlockSpec((BATCH, FFN_SUB * FFN_STEPS, D_MODEL), lambda i: (0, i, 0))}
    head = {"time": pl.BlockSpec((FFN_ROWS, D_MODEL), lambda i: (next_head(i), 0)),
            "batch": pl.BlockSpec((BATCH, FFN_STEPS, D_MODEL), lambda i: (0, next_head(i), 0))}
    shape = {"time": (N_ROWS, D_MODEL), "batch": (BATCH, SEQ, D_MODEL)}
    reorder = rows_in != rows_out
    slab = [pltpu.VMEM((D_MODEL // LANES, FFN_ROWS, LANES), F32)] if reorder else []
    return pl.pallas_call(
        functools.partial(_ffn_body, rows_in=rows_in, rows_out=rows_out, final_norm=final_norm),
        grid=(n_steps,),
        in_specs=[block[rows_in], head[rows_in],
                  const((None, None, 3, BATCH, D_MODEL), lambda i: (layer, sub, 0, 0, 0)),
                  const((None, 1, D_MODEL), lambda i: (layer, 0, 0)),
                  const((None, D_MODEL, 2 * D_FF), lambda i: (layer, 0, 0)),
                  const((None, D_FF, D_MODEL), lambda i: (layer, 0, 0)),
                  const((1, D_MODEL), lambda i: (0, 0))],
        out_specs=block[rows_out],
        out_shape=jax.ShapeDtypeStruct(shape[rows_out], F32),
        scratch_shapes=[pltpu.VMEM((3, FFN_ROWS, D_MODEL), BF16)] + slab,
        compiler_params=pltpu.CompilerParams(
            dimension_semantics=("arbitrary",), vmem_limit_bytes=VMEM_LIMIT),
        name=f"ffn_{rows_in}_to_{rows_out}" + ("_final" if final_norm else ""),
    )(x, x, cond, g, w_in, w_out, final_g)


def _group_norm(y, g):
    ms = jnp.mean(y * y, axis=-1, keepdims=True)
    return (y * lax.rsqrt(ms + EPS) * g).astype(BF16)


def _sgu_norm(za, seg_mean, v_hist):
    z = jax.nn.gelu(za)
    u, v = z[:, :W_GRP], z[:, W_GRP:]
    seg = lambda t: _dot(t.astype(BF16), seg_mean)
    d = v - seg(v)
    vn = d * lax.rsqrt(seg(d * d) + EPS)
    for s in range(W_GRP // LANES):
        v_hist[s, MIX_ROWS:, :] = vn[:, s * LANES:(s + 1) * LANES]
    return u


def _sgu_mix(sgu_w, sgu_bias, v_hist, m_slab):
    n_slab = W_GRP // LANES
    lane = lax.broadcasted_iota(jnp.int32, (CHUNK, BATCH * LANES), 1)
    first_head = jnp.bitwise_and(lane, LANES - 1) < SGU_HEAD_DIM
    for s in range(n_slab):
        cols = [v_hist[s, pl.ds(b, CHUNK, stride=BATCH), :] for b in range(BATCH)]
        vt = jnp.concatenate(cols, axis=1)
        rhs = jnp.concatenate([jnp.where(first_head, vt, 0.0), jnp.where(first_head, 0.0, vt)],
                              axis=0)
        mixed = _dot(sgu_w[s], rhs.astype(BF16))
        mixed = mixed + jnp.concatenate([sgu_bias[:, s * LANES:(s + 1) * LANES]] * BATCH, axis=1)
        for b in range(BATCH):
            m_slab[s, pl.ds(b, MIX_STEPS, stride=BATCH), :] = mixed[:, b * LANES:(b + 1) * LANES]
        v_hist[s, :MIX_ROWS, :] = v_hist[s, MIX_ROWS:, :]
    return jnp.concatenate([m_slab[s] for s in range(n_slab)], axis=1)


def _pool(zb, halo, t0, pool_w, pool_scale):
    e = jnp.concatenate([halo, zb], axis=0)
    shifted = lambda a, steps: a[:a.shape[0] - steps * BATCH]
    s2 = e[BATCH:] + shifted(e, 1)
    s4 = s2[2 * BATCH:] + shifted(s2, 2)
    s8 = s4[4 * BATCH:] + shifted(s4, 4)
    s16 = s8[8 * BATCH:] + shifted(s8, 8)
    tail = lambda a: a[a.shape[0] - MIX_ROWS:]
    lane = lax.broadcasted_iota(jnp.int32, (MIX_ROWS, W_GRP), 1)
    group = jnp.right_shift(lane, int(math.log2(POOL_GROUP_DIM)))
    win_sum = jnp.where(group == 0, tail(s2),
                        jnp.where(group == 1, tail(s4), jnp.where(group == 2, tail(s8), tail(s16))))
    window = jnp.left_shift(2, group)
    row = lax.broadcasted_iota(jnp.int32, (MIX_ROWS, W_GRP), 0)
    step = t0 + jnp.right_shift(row, int(math.log2(BATCH)))
    count = jnp.clip(step + 1, 1, window).astype(F32)
    p = win_sum / count - zb
    return _dot(p.astype(BF16), pool_w) * pool_scale


def _conv(zc, halo, conv_w):
    bg, cg, xh = zc[:, :W_GRP], zc[:, W_GRP:2 * W_GRP], zc[:, 2 * W_GRP:]
    y = cg * xh
    e = jnp.concatenate([halo, y], axis=0)
    out = (conv_w[0:1] * e[:MIX_ROWS] + conv_w[1:2] * e[BATCH:BATCH + MIX_ROWS] + conv_w[2:3] * y)
    return bg * out, y[MIX_ROWS - CONV_HALO:]


def _s5_scan(a_re, a_im, state_ref, scan_ref, t_lo, t_hi):
    ar = jnp.broadcast_to(a_re, (BATCH, N_STATE))
    ai = jnp.broadcast_to(a_im, (BATCH, N_STATE))
    st = state_ref[...]
    xr, xi = st[:, :N_STATE], st[:, N_STATE:]
    for t in range(t_lo, t_hi):
        rows = slice(t * BATCH, (t + 1) * BATCH)
        bu = scan_ref[rows, :]
        xr, xi = (ar * xr - ai * xi + bu[:, :N_STATE], ar * xi + ai * xr + bu[:, N_STATE:])
        scan_ref[rows, :] = jnp.concatenate([xr, xi], axis=1)
    state_ref[...] = jnp.concatenate([xr, xi], axis=1)


def _s5_out(zd, states, c_out, d_skip, glu_w, glu_b):
    y = _dot(states.astype(BF16), c_out) + d_skip * zd
    y = jax.nn.gelu(y)
    return y * jax.nn.sigmoid(_dot(y.astype(BF16), glu_w) + glu_b)


def _mix_body(x_ref, mod_ref, g_ref, win_ref, seg_ref, sguw_ref, sgub_ref, poolw_ref, pools_ref,
              convw_ref, are_ref, aim_ref, bin_ref, cout_ref, dskip_ref, gluw_ref, glub_ref,
              mg_ref, wout_ref, o_ref,
              z_even, z_odd, x_even, x_odd, pool_halo, conv_halo, state_ref, scan_ref, v_hist,
              m_slab):
    i = pl.program_id(0)

    @pl.when(i == 0)
    def _():
        for ref in (z_odd, x_odd, pool_halo, conv_halo, state_ref, v_hist):
            ref[...] = jnp.zeros_like(ref)

    def step(z_a, x_a, z_b, x_b, parity_b):
        x = x_ref[...]
        h = _modulate(_rmsnorm(x, g_ref[...]), mod_ref[0], mod_ref[1]).astype(BF16)
        x_a[...] = x

        def project(k):
            cols = slice(k * W_GRP, (k + 1) * W_GRP)
            z_a[:, cols] = _dot(h, win_ref[:, cols])

        mg = mg_ref[...]
        a_re, a_im = are_ref[...], aim_ref[...]
        zd = z_b[:, 6 * W_GRP:]
        scan_ref[...] = _dot(zd.astype(BF16), bin_ref[...])
        project(0)
        u = _sgu_norm(z_b[:, :2 * W_GRP], seg_ref[...], v_hist)
        project(1)
        ya = u * _sgu_mix(sguw_ref[parity_b], sgub_ref[parity_b], v_hist, m_slab)
        na = _group_norm(ya, mg[:, :W_GRP])
        project(2)
        _s5_scan(a_re, a_im, state_ref, scan_ref, 0, MIX_STEPS // 2)
        project(3)
        zb = z_b[:, 2 * W_GRP:3 * W_GRP]
        yb = _pool(zb, pool_halo[...], (i - 1) * MIX_STEPS, poolw_ref[...], pools_ref[...])
        pool_halo[...] = zb[MIX_ROWS - POOL_HALO:]
        nb = _group_norm(yb, mg[:, W_GRP:2 * W_GRP])
        project(4)
        _s5_scan(a_re, a_im, state_ref, scan_ref, MIX_STEPS // 2, MIX_STEPS)
        project(5)
        yc, new_halo = _conv(z_b[:, 3 * W_GRP:6 * W_GRP], conv_halo[...], convw_ref[...])
        conv_halo[...] = new_halo
        nc = _group_norm(yc, mg[:, 2 * W_GRP:3 * W_GRP])
        project(6)
        yd = _s5_out(zd, scan_ref[...], cout_ref[...], dskip_ref[...], gluw_ref[...],
                     glub_ref[...])
        nd = _group_norm(yd, mg[:, 3 * W_GRP:])

        y = jnp.concatenate([na, nb, nc, nd], axis=1)
        out = _dot(y, wout_ref[...])
        o_ref[...] = _gated_residual(x_b[...], mod_ref[2], out)

    @pl.when(i % 2 == 0)
    def _():
        step(z_even, x_even, z_odd, x_odd, parity_b=1)

    @pl.when(i % 2 == 1)
    def _():
        step(z_odd, x_odd, z_even, x_even, parity_b=0)


def _mix_call(x, cond, g, p, *, layer):
    def layer_block(a):
        zeros = (0,) * (a.ndim - 1)
        return pl.BlockSpec((None,) + a.shape[1:], lambda i: (layer,) + zeros,
                            pipeline_mode=pl.Buffered(1))
    n_tiles = N_ROWS // MIX_ROWS
    operands = [g, p["w_in"], p["seg_mean"], p["sgu_w"], p["sgu_bias"], p["pool_w"],
                p["pool_scale"], p["conv_w"], p["a_re"], p["a_im"], p["b_in"], p["c_out"],
                p["d_skip"], p["glu_w"], p["glu_b"], p["mix_g"], p["w_out"]]
    cond_spec = pl.BlockSpec((None, None, 3, BATCH, D_MODEL), lambda i: (layer, 1, 0, 0, 0),
                             pipeline_mode=pl.Buffered(1))
    return pl.pallas_call(
        _mix_body,
        grid=(n_tiles + 1,),
        in_specs=[pl.BlockSpec((MIX_ROWS, D_MODEL), lambda i: (jnp.minimum(i, n_tiles - 1), 0)),
                  cond_spec] + [layer_block(o) for o in operands],
        out_specs=pl.BlockSpec((MIX_ROWS, D_MODEL), lambda i: (jnp.maximum(i - 1, 0), 0)),
        out_shape=jax.ShapeDtypeStruct((N_ROWS, D_MODEL), F32),
        scratch_shapes=[
            pltpu.VMEM((MIX_ROWS, P_IN), F32),
            pltpu.VMEM((MIX_ROWS, P_IN), F32),
            pltpu.VMEM((MIX_ROWS, D_MODEL), F32),
            pltpu.VMEM((MIX_ROWS, D_MODEL), F32),
            pltpu.VMEM((POOL_HALO, W_GRP), F32),
            pltpu.VMEM((CONV_HALO, W_GRP), F32),
            pltpu.VMEM((BATCH, 2 * N_STATE), F32),
            pltpu.VMEM((MIX_ROWS, 2 * N_STATE), F32),
            pltpu.VMEM((W_GRP // LANES, 2 * MIX_ROWS, LANES), F32),
            pltpu.VMEM((W_GRP // LANES, MIX_ROWS, LANES), F32),
        ],
        compiler_params=pltpu.CompilerParams(
            dimension_semantics=("arbitrary",), vmem_limit_bytes=VMEM_LIMIT),
        name="mixers",
    )(x, cond, *operands)


def _block_diag(blocks):
    l, g, r, c = blocks.shape
    eye = jnp.eye(g, dtype=blocks.dtype)
    return (blocks[:, :, :, None, :] * eye[None, :, None, :, None]).reshape(l, g * r, g * c)


def _mixer_params(w_mix_in, sgu_w, sgu_b, pool_w, pool_scale, conv_w, a_re, a_im, bb_re, bb_im,
                  s5_c_re, s5_c_im, s5_d, s5_glu_w, s5_glu_b, mix_norm_g, w_mix_out):
    tril = jnp.tril(jnp.ones((CHUNK, CHUNK), F32))
    w_s = sgu_w * tril
    first = jnp.concatenate([jnp.zeros_like(w_s[:, :, :MIX_STEPS, :MIX_STEPS]),
                             w_s[:, :, :MIX_STEPS, :MIX_STEPS]], axis=3)
    w_half = jnp.stack([first, w_s[:, :, MIX_STEPS:, :]], axis=1)
    sgu_pair = jnp.concatenate([w_half[:, :, 0::2], w_half[:, :, 1::2]], axis=4).astype(BF16)
    sgu_bias = jnp.repeat(jnp.swapaxes(sgu_b, 1, 2), SGU_HEAD_DIM, axis=2)
    sgu_bias = sgu_bias.reshape(DEPTH, 2, MIX_STEPS, W_GRP)
    seg_mean = _block_diag(jnp.full((DEPTH, SGU_HEADS, SGU_HEAD_DIM, SGU_HEAD_DIM),
                                    1.0 / SGU_HEAD_DIM, F32)).astype(BF16)
    to_blocks = lambda bb: jnp.transpose(
        bb.reshape(DEPTH, S5_GROUP_CH, S5_GROUPS, S5_STATE), (0, 2, 1, 3))
    b_in = jnp.concatenate([_block_diag(to_blocks(bb_re)), _block_diag(to_blocks(bb_im))],
                           axis=2).astype(BF16)
    c_t = lambda c: jnp.swapaxes(c, 2, 3)
    c_out = jnp.concatenate([_block_diag(c_t(s5_c_re)), -_block_diag(c_t(s5_c_im))],
                            axis=1).astype(BF16)
    row = lambda v: v.reshape(DEPTH, 1, -1)
    return {
        "w_in": w_mix_in.astype(BF16), "seg_mean": seg_mean, "sgu_w": sgu_pair,
        "sgu_bias": sgu_bias, "pool_w": _block_diag(pool_w).astype(BF16),
        "pool_scale": row(pool_scale), "conv_w": conv_w, "a_re": a_re, "a_im": a_im,
        "b_in": b_in, "c_out": c_out, "d_skip": row(s5_d), "glu_w": s5_glu_w.astype(BF16),
        "glu_b": row(s5_glu_b), "mix_g": row(mix_norm_g), "w_out": w_mix_out.astype(BF16),
    }


def kernel(x, c, ada_w, ada_b, norm1_g, ffn1_w_in, ffn1_w_out, norm2_g, w_mix_in, sgu_w, sgu_b, pool_w, pool_scale, conv_w, s5_lambda_re, s5_lambda_im, s5_b_re, s5_b_im, s5_c_re, s5_c_im, s5_d, s5_log_dt, s5_glu_w, s5_glu_b, mix_norm_g, w_mix_out, norm3_g, ffn2_w_in, ffn2_w_out, final_norm_g):
    cond = _ada_call(c, ada_w, ada_b).reshape(DEPTH, BATCH, 3, 3, D_MODEL)
    cond = jnp.transpose(cond, (0, 2, 3, 1, 4))
    a_re, a_im, bb_re, bb_im = _s5_disc_call(s5_lambda_re, s5_lambda_im, s5_log_dt, s5_b_re, s5_b_im)
    p = _mixer_params(w_mix_in, sgu_w, sgu_b, pool_w, pool_scale, conv_w, a_re, a_im, bb_re, bb_im,
                      s5_c_re, s5_c_im, s5_d, s5_glu_w, s5_glu_b, mix_norm_g, w_mix_out)
    row = lambda v: v.reshape(DEPTH, 1, D_MODEL)
    g1, g2, g3 = row(norm1_g), row(norm2_g), row(norm3_g)
    final_g = final_norm_g.reshape(1, D_MODEL)
    w1_in, w1_out = ffn1_w_in.astype(BF16), ffn1_w_out.astype(BF16)
    w2_in, w2_out = ffn2_w_in.astype(BF16), ffn2_w_out.astype(BF16)

    xs = x
    for l in range(DEPTH):
        last = l == DEPTH - 1
        xs = _ffn_call(xs, cond, g1, w1_in, w1_out, final_g, layer=l, sub=0,
                       rows_in="batch" if l == 0 else "time")
        xs = _mix_call(xs, cond, g2, p, layer=l)
        xs = _ffn_call(xs, cond, g3, w2_in, w2_out, final_g, layer=l, sub=2,
                       rows_out="batch" if last else "time", final_norm=last)
    return xs
```

```python
import functools
import math

import jax
import jax.numpy as jnp
from jax import lax
from jax.experimental import pallas as pl
from jax.experimental.pallas import tpu as pltpu

F32 = jnp.float32
BF16 = jnp.bfloat16

D_MODEL = 1024
BATCH = 8
SEQ = 2048
DEPTH = 4
W_GRP = 256
SGU_HEADS = 4
SGU_HEAD_DIM = 64
CHUNK = 128
POOL_WINDOWS = (2, 4, 8, 16)
POOL_GROUP_DIM = 64
S5_GROUP_CH = 16
S5_GROUPS = 16
S5_STATE = 64
N_STATE = S5_GROUPS * S5_STATE
P_IN = 7 * W_GRP
D_FF = 2816
N_ADA = 9
EPS = 1e-6

N_ROWS = SEQ * BATCH
SUBLANES = 8
LANES = 128
VMEM_LIMIT = 56 * 1024 * 1024

FFN_ROWS = 512
FFN_CHUNK = 256
FFN_STEPS = FFN_ROWS // BATCH
MIX_STEPS = CHUNK // 2
MIX_ROWS = MIX_STEPS * BATCH
POOL_HALO = 16 * BATCH
CONV_HALO = 2 * BATCH
ADA_COLS = 1152


def _dot(a, b):
    return jnp.dot(a, b, preferred_element_type=F32)


def _split_bf16(x):
    hi = x.astype(BF16)
    lo = (x - hi.astype(F32)).astype(BF16)
    return hi, lo


def _rmsnorm(x, g):
    ms = jnp.mean(x * x, axis=-1, keepdims=True)
    return x * lax.rsqrt(ms + EPS) * g


def _per_batch(x, fn, batch_major=False):
    rows, c = x.shape
    if batch_major:
        return fn(x.reshape(BATCH, rows // BATCH, c), lambda p: p[:, None, :]).reshape(rows, c)
    return fn(x.reshape(rows // BATCH, BATCH, c), lambda p: p[None]).reshape(rows, c)


def _modulate(y, shift, scale, batch_major=False):
    return _per_batch(y, lambda y3, ex: y3 * ex(1.0 + scale) + ex(shift), batch_major)


def _gated_residual(x, gate, y, batch_major=False):
    return x + _per_batch(y, lambda y3, ex: y3 * ex(gate), batch_major)


def _ada_body(c_ref, w_ref, b_ref, o_ref):
    c = c_ref[...]
    c_act = c * jax.nn.sigmoid(c)
    c_hi, c_lo = _split_bf16(c_act)
    w_hi, w_lo = _split_bf16(w_ref[...])
    o_ref[...] = _dot(c_hi, w_hi) + (_dot(c_lo, w_hi) + _dot(c_hi, w_lo)) + b_ref[...]


def _ada_call(c, ada_w, ada_b):
    n_col = N_ADA * D_MODEL // ADA_COLS
    return pl.pallas_call(
        _ada_body,
        grid=(DEPTH, n_col),
        in_specs=[
            pl.BlockSpec((BATCH, D_MODEL), lambda l, j: (0, 0)),
            pl.BlockSpec((None, D_MODEL, ADA_COLS), lambda l, j: (l, 0, j)),
            pl.BlockSpec((None, 1, ADA_COLS), lambda l, j: (l, 0, j)),
        ],
        out_specs=pl.BlockSpec((None, BATCH, ADA_COLS), lambda l, j: (l, 0, j)),
        out_shape=jax.ShapeDtypeStruct((DEPTH, BATCH, N_ADA * D_MODEL), F32),
        compiler_params=pltpu.CompilerParams(
            dimension_semantics=("arbitrary", "arbitrary"), vmem_limit_bytes=VMEM_LIMIT),
        name="ada_cond",
    )(c, ada_w, ada_b.reshape(DEPTH, 1, N_ADA * D_MODEL))


def _s5_disc_body(lre_ref, lim_ref, ldt_ref, br_ref, bi_ref, are_ref, aim_ref, bbr_ref, bbi_ref):
    lre, lim = lre_ref[...], lim_ref[...]
    dt = jnp.exp(ldt_ref[...])
    mag = jnp.exp(lre * dt)
    ang = lim * dt
    a_re, a_im = mag * jnp.cos(ang), mag * jnp.sin(ang)
    nr, ni = a_re - 1.0, a_im
    den = lre * lre + lim * lim
    k_re = (nr * lre + ni * lim) / den
    k_im = (ni * lre - nr * lim) / den
    br, bi = br_ref[...], bi_ref[...]
    are_ref[...] = a_re
    aim_ref[...] = a_im
    bbr_ref[...] = k_re * br - k_im * bi
    bbi_ref[...] = k_re * bi + k_im * br


def _s5_disc_call(lam_re, lam_im, log_dt, b_re, b_im):
    row = lambda a: a.reshape(DEPTH, 1, N_STATE)
    chan_major = lambda b: jnp.transpose(b, (0, 3, 1, 2)).reshape(DEPTH, S5_GROUP_CH, N_STATE)
    ldt = jnp.repeat(log_dt, S5_STATE, axis=-1)
    vec = pl.BlockSpec((None, 1, N_STATE), lambda l: (l, 0, 0))
    mat = pl.BlockSpec((None, S5_GROUP_CH, N_STATE), lambda l: (l, 0, 0))
    return pl.pallas_call(
        _s5_disc_body,
        grid=(DEPTH,),
        in_specs=[vec, vec, vec, mat, mat],
        out_specs=[vec, vec, mat, mat],
        out_shape=[jax.ShapeDtypeStruct((DEPTH, 1, N_STATE), F32)] * 2
        + [jax.ShapeDtypeStruct((DEPTH, S5_GROUP_CH, N_STATE), F32)] * 2,
        name="s5_discretise",
    )(row(lam_re), row(lam_im), row(ldt), chan_major(b_re), chan_major(b_im))


def _ffn_body(x_ref, mod_ref, g_ref, win_ref, wout_ref, fg_ref, o_ref, *slab, rows_in, rows_out,
              final_norm):
    batch_major = rows_in == "batch"
    x = x_ref[...].reshape(FFN_ROWS, D_MODEL)
    h = _modulate(_rmsnorm(x, g_ref[...]), mod_ref[0], mod_ref[1], batch_major)
    acc = None
    for j in range(D_FF // FFN_CHUNK):
        lo = j * FFN_CHUNK
        a = _dot(h, win_ref[:, lo:lo + FFN_CHUNK])
        b = _dot(h, win_ref[:, D_FF + lo:D_FF + lo + FFN_CHUNK])
        gated = a * jax.nn.sigmoid(a) * b
        part = _dot(gated, wout_ref[lo:lo + FFN_CHUNK, :])
        acc = part if acc is None else acc + part
    y = _gated_residual(x, 0.5 * mod_ref[2], acc, batch_major)
    if final_norm:
        y = _rmsnorm(y, fg_ref[...])
    if rows_in == rows_out:
        o_ref[...] = y.reshape(o_ref.shape)
        return
    (slab,) = slab
    n_slab = D_MODEL // LANES
    lanes = lambda s: slice(s * LANES, (s + 1) * LANES)
    batch_rows = lambda b: pl.ds(b, FFN_STEPS, stride=BATCH)
    if rows_out == "time":
        for s in range(n_slab):
            for b in range(BATCH):
                slab[s, batch_rows(b), :] = y[b * FFN_STEPS:(b + 1) * FFN_STEPS, lanes(s)]
        for s in range(n_slab):
            o_ref[:, lanes(s)] = slab[s]
    else:
        for s in range(n_slab):
            slab[s] = y[:, lanes(s)]
        for b in range(BATCH):
            for s in range(n_slab):
                o_ref[b, :, lanes(s)] = slab[s, batch_rows(b), :]


def _ffn_call(x, cond, g, w_in, w_out, final_g, *, layer, sub, rows_in="time", rows_out="time",
              final_norm=False):
    const = functools.partial(pl.BlockSpec, pipeline_mode=pl.Buffered(1))
    time_rows = pl.BlockSpec((FFN_ROWS, D_MODEL), lambda i: (i, 0))
    batch_rows = pl.BlockSpec((BATCH, FFN_STEPS, D_MODEL), lambda i: (0, i, 0))
    spec = {"time": time_rows, "batch": batch_rows}
    shape = {"time": (N_ROWS, D_MODEL), "batch": (BATCH, SEQ, D_MODEL)}
    reorder = rows_in != rows_out
    return pl.pallas_call(
        functools.partial(_ffn_body, rows_in=rows_in, rows_out=rows_out, final_norm=final_norm),
        grid=(N_ROWS // FFN_ROWS,),
        in_specs=[spec[rows_in],
                  const((None, None, 3, BATCH, D_MODEL), lambda i: (layer, sub, 0, 0, 0)),
                  const((None, 1, D_MODEL), lambda i: (layer, 0, 0)),
                  const((None, D_MODEL, 2 * D_FF), lambda i: (layer, 0, 0)),
                  const((None, D_FF, D_MODEL), lambda i: (layer, 0, 0)),
                  const((1, D_MODEL), lambda i: (0, 0))],
        out_specs=spec[rows_out],
        out_shape=jax.ShapeDtypeStruct(shape[rows_out], F32),
        scratch_shapes=[pltpu.VMEM((D_MODEL // LANES, FFN_ROWS, LANES), F32)] if reorder else [],
        compiler_params=pltpu.CompilerParams(
            dimension_semantics=("arbitrary",), vmem_limit_bytes=VMEM_LIMIT),
        name=f"ffn_{rows_in}_to_{rows_out}" + ("_final" if final_norm else ""),
    )(x, cond, g, w_in, w_out, final_g)


def _group_norm(y, g):
    ms = jnp.mean(y * y, axis=-1, keepdims=True)
    return (y * lax.rsqrt(ms + EPS) * g).astype(BF16)


def _sgu_norm(za, seg_mean, v_hist):
    z = jax.nn.gelu(za)
    u, v = z[:, :W_GRP], z[:, W_GRP:]
    seg = lambda t: _dot(t.astype(BF16), seg_mean)
    d = v - seg(v)
    vn = d * lax.rsqrt(seg(d * d) + EPS)
    for s in range(W_GRP // LANES):
        v_hist[s, MIX_ROWS:, :] = vn[:, s * LANES:(s + 1) * LANES]
    return u


def _sgu_mix(sgu_w, sgu_bias, v_hist, m_slab):
    n_slab = W_GRP // LANES
    lane = lax.broadcasted_iota(jnp.int32, (CHUNK, BATCH * LANES), 1)
    first_head = jnp.bitwise_and(lane, LANES - 1) < SGU_HEAD_DIM
    for s in range(n_slab):
        cols = [v_hist[s, pl.ds(b, CHUNK, stride=BATCH), :] for b in range(BATCH)]
        vt = jnp.concatenate(cols, axis=1)
        rhs = jnp.concatenate([jnp.where(first_head, vt, 0.0), jnp.where(first_head, 0.0, vt)],
                              axis=0)
        mixed = _dot(sgu_w[s], rhs.astype(BF16))
        mixed = mixed + jnp.concatenate([sgu_bias[:, s * LANES:(s + 1) * LANES]] * BATCH, axis=1)
        for b in range(BATCH):
            m_slab[s, pl.ds(b, MIX_STEPS, stride=BATCH), :] = mixed[:, b * LANES:(b + 1) * LANES]
        v_hist[s, :MIX_ROWS, :] = v_hist[s, MIX_ROWS:, :]
    return jnp.concatenate([m_slab[s] for s in range(n_slab)], axis=1)


def _pool(zb, halo, t0, pool_w, pool_scale):
    e = jnp.concatenate([halo, zb], axis=0)
    shifted = lambda a, steps: a[:a.shape[0] - steps * BATCH]
    s2 = e[BATCH:] + shifted(e, 1)
    s4 = s2[2 * BATCH:] + shifted(s2, 2)
    s8 = s4[4 * BATCH:] + shifted(s4, 4)
    s16 = s8[8 * BATCH:] + shifted(s8, 8)
    tail = lambda a: a[a.shape[0] - MIX_ROWS:]
    lane = lax.broadcasted_iota(jnp.int32, (MIX_ROWS, W_GRP), 1)
    group = jnp.right_shift(lane, int(math.log2(POOL_GROUP_DIM)))
    win_sum = jnp.where(group == 0, tail(s2),
                        jnp.where(group == 1, tail(s4), jnp.where(group == 2, tail(s8), tail(s16))))
    window = jnp.left_shift(2, group)
    row = lax.broadcasted_iota(jnp.int32, (MIX_ROWS, W_GRP), 0)
    step = t0 + jnp.right_shift(row, int(math.log2(BATCH)))
    count = jnp.clip(step + 1, 1, window).astype(F32)
    p = win_sum / count - zb
    return _dot(p.astype(BF16), pool_w) * pool_scale


def _conv(zc, halo, conv_w):
    bg, cg, xh = zc[:, :W_GRP], zc[:, W_GRP:2 * W_GRP], zc[:, 2 * W_GRP:]
    y = cg * xh
    e = jnp.concatenate([halo, y], axis=0)
    out = (conv_w[0:1] * e[:MIX_ROWS] + conv_w[1:2] * e[BATCH:BATCH + MIX_ROWS] + conv_w[2:3] * y)
    return bg * out, y[MIX_ROWS - CONV_HALO:]


def _s5_scan(a_re, a_im, state_ref, scan_ref, t_lo, t_hi):
    ar = jnp.broadcast_to(a_re, (BATCH, N_STATE))
    ai = jnp.broadcast_to(a_im, (BATCH, N_STATE))
    st = state_ref[...]
    xr, xi = st[:, :N_STATE], st[:, N_STATE:]
    for t in range(t_lo, t_hi):
        rows = slice(t * BATCH, (t + 1) * BATCH)
        bu = scan_ref[rows, :]
        xr, xi = (ar * xr - ai * xi + bu[:, :N_STATE], ar * xi + ai * xr + bu[:, N_STATE:])
        scan_ref[rows, :] = jnp.concatenate([xr, xi], axis=1)
    state_ref[...] = jnp.concatenate([xr, xi], axis=1)


def _s5_out(zd, states, c_out, d_skip, glu_w, glu_b):
    y = _dot(states.astype(BF16), c_out) + d_skip * zd
    y = jax.nn.gelu(y)
    return y * jax.nn.sigmoid(_dot(y.astype(BF16), glu_w) + glu_b)


def _mix_body(x_ref, mod_ref, g_ref, win_ref, seg_ref, sguw_ref, sgub_ref, poolw_ref, pools_ref,
              convw_ref, are_ref, aim_ref, bin_ref, cout_ref, dskip_ref, gluw_ref, glub_ref,
              mg_ref, wout_ref, o_ref,
              z_even, z_odd, x_even, x_odd, pool_halo, conv_halo, state_ref, scan_ref, v_hist,
              m_slab):
    i = pl.program_id(0)

    @pl.when(i == 0)
    def _():
        for ref in (z_odd, x_odd, pool_halo, conv_halo, state_ref, v_hist):
            ref[...] = jnp.zeros_like(ref)

    def step(z_a, x_a, z_b, x_b, parity_b):
        x = x_ref[...]
        h = _modulate(_rmsnorm(x, g_ref[...]), mod_ref[0], mod_ref[1]).astype(BF16)
        x_a[...] = x

        def project(k):
            cols = slice(k * W_GRP, (k + 1) * W_GRP)
            z_a[:, cols] = _dot(h, win_ref[:, cols])

        mg = mg_ref[...]
        a_re, a_im = are_ref[...], aim_ref[...]
        zd = z_b[:, 6 * W_GRP:]
        scan_ref[...] = _dot(zd.astype(BF16), bin_ref[...])
        project(0)
        u = _sgu_norm(z_b[:, :2 * W_GRP], seg_ref[...], v_hist)
        project(1)
        ya = u * _sgu_mix(sguw_ref[parity_b], sgub_ref[parity_b], v_hist, m_slab)
        na = _group_norm(ya, mg[:, :W_GRP])
        project(2)
        _s5_scan(a_re, a_im, state_ref, scan_ref, 0, MIX_STEPS // 2)
        project(3)
        zb = z_b[:, 2 * W_GRP:3 * W_GRP]
        yb = _pool(zb, pool_halo[...], (i - 1) * MIX_STEPS, poolw_ref[...], pools_ref[...])
        pool_halo[...] = zb[MIX_ROWS - POOL_HALO:]
        nb = _group_norm(yb, mg[:, W_GRP:2 * W_GRP])
        project(4)
        _s5_scan(a_re, a_im, state_ref, scan_ref, MIX_STEPS // 2, MIX_STEPS)
        project(5)
        yc, new_halo = _conv(z_b[:, 3 * W_GRP:6 * W_GRP], conv_halo[...], convw_ref[...])
        conv_halo[...] = new_halo
        nc = _group_norm(yc, mg[:, 2 * W_GRP:3 * W_GRP])
        project(6)
        yd = _s5_out(zd, scan_ref[...], cout_ref[...], dskip_ref[...], gluw_ref[...],
                     glub_ref[...])
        nd = _group_norm(yd, mg[:, 3 * W_GRP:])

        y = jnp.concatenate([na, nb, nc, nd], axis=1)
        out = _dot(y, wout_ref[...])
        o_ref[...] = _gated_residual(x_b[...], mod_ref[2], out)

    @pl.when(i % 2 == 0)
    def _():
        step(z_even, x_even, z_odd, x_odd, parity_b=1)

    @pl.when(i % 2 == 1)
    def _():
        step(z_odd, x_odd, z_even, x_even, parity_b=0)


def _mix_call(x, cond, g, p, *, layer):
    def layer_block(a):
        zeros = (0,) * (a.ndim - 1)
        return pl.BlockSpec((None,) + a.shape[1:], lambda i: (layer,) + zeros,
                            pipeline_mode=pl.Buffered(1))
    n_tiles = N_ROWS // MIX_ROWS
    operands = [g, p["w_in"], p["seg_mean"], p["sgu_w"], p["sgu_bias"], p["pool_w"],
                p["pool_scale"], p["conv_w"], p["a_re"], p["a_im"], p["b_in"], p["c_out"],
                p["d_skip"], p["glu_w"], p["glu_b"], p["mix_g"], p["w_out"]]
    cond_spec = pl.BlockSpec((None, None, 3, BATCH, D_MODEL), lambda i: (layer, 1, 0, 0, 0),
                             pipeline_mode=pl.Buffered(1))
    return pl.pallas_call(
        _mix_body,
        grid=(n_tiles + 1,),
        in_specs=[pl.BlockSpec((MIX_ROWS, D_MODEL), lambda i: (jnp.minimum(i, n_tiles - 1), 0)),
                  cond_spec] + [layer_block(o) for o in operands],
        out_specs=pl.BlockSpec((MIX_ROWS, D_MODEL), lambda i: (jnp.maximum(i - 1, 0), 0)),
        out_shape=jax.ShapeDtypeStruct((N_ROWS, D_MODEL), F32),
        scratch_shapes=[
            pltpu.VMEM((MIX_ROWS, P_IN), F32),
            pltpu.VMEM((MIX_ROWS, P_IN), F32),
            pltpu.VMEM((MIX_ROWS, D_MODEL), F32),
            pltpu.VMEM((MIX_ROWS, D_MODEL), F32),
            pltpu.VMEM((POOL_HALO, W_GRP), F32),
            pltpu.VMEM((CONV_HALO, W_GRP), F32),
            pltpu.VMEM((BATCH, 2 * N_STATE), F32),
            pltpu.VMEM((MIX_ROWS, 2 * N_STATE), F32),
            pltpu.VMEM((W_GRP // LANES, 2 * MIX_ROWS, LANES), F32),
            pltpu.VMEM((W_GRP // LANES, MIX_ROWS, LANES), F32),
        ],
        compiler_params=pltpu.CompilerParams(
            dimension_semantics=("arbitrary",), vmem_limit_bytes=VMEM_LIMIT),
        name="mixers",
    )(x, cond, *operands)


def _block_diag(blocks):
    l, g, r, c = blocks.shape
    eye = jnp.eye(g, dtype=blocks.dtype)
    return (blocks[:, :, :, None, :] * eye[None, :, None, :, None]).reshape(l, g * r, g * c)


def _mixer_params(w_mix_in, sgu_w, sgu_b, pool_w, pool_scale, conv_w, a_re, a_im, bb_re, bb_im,
                  s5_c_re, s5_c_im, s5_d, s5_glu_w, s5_glu_b, mix_norm_g, w_mix_out):
    tril = jnp.tril(jnp.ones((CHUNK, CHUNK), F32))
    w_s = sgu_w * tril
    first = jnp.concatenate([jnp.zeros_like(w_s[:, :, :MIX_STEPS, :MIX_STEPS]),
                             w_s[:, :, :MIX_STEPS, :MIX_STEPS]], axis=3)
    w_half = jnp.stack([first, w_s[:, :, MIX_STEPS:, :]], axis=1)
    sgu_pair = jnp.concatenate([w_half[:, :, 0::2], w_half[:, :, 1::2]], axis=4).astype(BF16)
    sgu_bias = jnp.repeat(jnp.swapaxes(sgu_b, 1, 2), SGU_HEAD_DIM, axis=2)
    sgu_bias = sgu_bias.reshape(DEPTH, 2, MIX_STEPS, W_GRP)
    seg_mean = _block_diag(jnp.full((DEPTH, SGU_HEADS, SGU_HEAD_DIM, SGU_HEAD_DIM),
                                    1.0 / SGU_HEAD_DIM, F32)).astype(BF16)
    to_blocks = lambda bb: jnp.transpose(
        bb.reshape(DEPTH, S5_GROUP_CH, S5_GROUPS, S5_STATE), (0, 2, 1, 3))
    b_in = jnp.concatenate([_block_diag(to_blocks(bb_re)), _block_diag(to_blocks(bb_im))],
                           axis=2).astype(BF16)
    c_t = lambda c: jnp.swapaxes(c, 2, 3)
    c_out = jnp.concatenate([_block_diag(c_t(s5_c_re)), -_block_diag(c_t(s5_c_im))],
                            axis=1).astype(BF16)
    row = lambda v: v.reshape(DEPTH, 1, -1)
    return {
        "w_in": w_mix_in.astype(BF16), "seg_mean": seg_mean, "sgu_w": sgu_pair,
        "sgu_bias": sgu_bias, "pool_w": _block_diag(pool_w).astype(BF16),
        "pool_scale": row(pool_scale), "conv_w": conv_w, "a_re": a_re, "a_im": a_im,
        "b_in": b_in, "c_out": c_out, "d_skip": row(s5_d), "glu_w": s5_glu_w.astype(BF16),
        "glu_b": row(s5_glu_b), "mix_g": row(mix_norm_g), "w_out": w_mix_out.astype(BF16),
    }


def kernel(x, c, ada_w, ada_b, norm1_g, ffn1_w_in, ffn1_w_out, norm2_g, w_mix_in, sgu_w, sgu_b, pool_w, pool_scale, conv_w, s5_lambda_re, s5_lambda_im, s5_b_re, s5_b_im, s5_c_re, s5_c_im, s5_d, s5_log_dt, s5_glu_w, s5_glu_b, mix_norm_g, w_mix_out, norm3_g, ffn2_w_in, ffn2_w_out, final_norm_g):
    cond = _ada_call(c, ada_w, ada_b).reshape(DEPTH, BATCH, 3, 3, D_MODEL)
    cond = jnp.transpose(cond, (0, 2, 3, 1, 4))
    a_re, a_im, bb_re, bb_im = _s5_disc_call(s5_lambda_re, s5_lambda_im, s5_log_dt, s5_b_re, s5_b_im)
    p = _mixer_params(w_mix_in, sgu_w, sgu_b, pool_w, pool_scale, conv_w, a_re, a_im, bb_re, bb_im,
                      s5_c_re, s5_c_im, s5_d, s5_glu_w, s5_glu_b, mix_norm_g, w_mix_out)
    row = lambda v: v.reshape(DEPTH, 1, D_MODEL)
    g1, g2, g3 = row(norm1_g), row(norm2_g), row(norm3_g)
    final_g = final_norm_g.reshape(1, D_MODEL)

    xs = x
    for l in range(DEPTH):
        last = l == DEPTH - 1
        xs = _ffn_call(xs, cond, g1, ffn1_w_in, ffn1_w_out, final_g, layer=l, sub=0,
                       rows_in="batch" if l == 0 else "time")
        xs = _mix_call(xs, cond, g2, p, layer=l)
        xs = _ffn_call(xs, cond, g3, ffn2_w_in, ffn2_w_out, final_g, layer=l, sub=2,
                       rows_out="batch" if last else "time", final_norm=last)
    return xs
```

```python
import functools
import math

import jax
import jax.numpy as jnp
from jax import lax
from jax.experimental import pallas as pl
from jax.experimental.pallas import tpu as pltpu

F32 = jnp.float32
BF16 = jnp.bfloat16

D_MODEL = 1024
BATCH = 8
SEQ = 2048
DEPTH = 4
W_GRP = 256
SGU_HEADS = 4
SGU_HEAD_DIM = 64
CHUNK = 128
POOL_WINDOWS = (2, 4, 8, 16)
POOL_GROUP_DIM = 64
S5_GROUP_CH = 16
S5_GROUPS = 16
S5_STATE = 64
N_STATE = S5_GROUPS * S5_STATE
P_IN = 7 * W_GRP
D_FF = 2816
N_ADA = 9
EPS = 1e-6

N_ROWS = SEQ * BATCH
SUBLANES = 8
LANES = 128
VMEM_LIMIT = 56 * 1024 * 1024

FFN_ROWS = 512
FFN_CHUNK = 256
FFN_STEPS = FFN_ROWS // BATCH
MIX_STEPS = CHUNK // 2
MIX_ROWS = MIX_STEPS * BATCH
POOL_HALO = 16 * BATCH
CONV_HALO = 2 * BATCH
ADA_PARTS = 3


def _dot(a, b):
    return jnp.dot(a, b, preferred_element_type=F32)


def _split_bf16(x):
    hi = x.astype(BF16)
    lo = (x - hi.astype(F32)).astype(BF16)
    return hi, lo


def _rmsnorm(x, g):
    ms = jnp.mean(x * x, axis=-1, keepdims=True)
    return x * lax.rsqrt(ms + EPS) * g


def _per_batch(x, fn, batch_major=False):
    rows, c = x.shape
    if batch_major:
        return fn(x.reshape(BATCH, rows // BATCH, c), lambda p: p[:, None, :]).reshape(rows, c)
    return fn(x.reshape(rows // BATCH, BATCH, c), lambda p: p[None]).reshape(rows, c)


def _modulate(y, shift, scale, batch_major=False):
    return _per_batch(y, lambda y3, ex: y3 * ex(1.0 + scale) + ex(shift), batch_major)


def _gated_residual(x, gate, y, batch_major=False):
    return x + _per_batch(y, lambda y3, ex: y3 * ex(gate), batch_major)


def _ada_body(c_ref, w_ref, b_ref, o_ref):
    c = c_ref[...]
    c_act = c * jax.nn.sigmoid(c)
    c_hi, c_lo = _split_bf16(c_act)
    w_hi, w_lo = _split_bf16(w_ref[...])
    cond = _dot(c_hi, w_hi) + (_dot(c_lo, w_hi) + _dot(c_hi, w_lo)) + b_ref[...]
    for k in range(ADA_PARTS):
        o_ref[k] = cond[:, k * D_MODEL:(k + 1) * D_MODEL]


def _ada_call(c, ada_w, ada_b):
    cols = ADA_PARTS * D_MODEL
    return pl.pallas_call(
        _ada_body,
        grid=(DEPTH, N_ADA // ADA_PARTS),
        in_specs=[
            pl.BlockSpec((BATCH, D_MODEL), lambda l, j: (0, 0)),
            pl.BlockSpec((None, D_MODEL, cols), lambda l, j: (l, 0, j)),
            pl.BlockSpec((None, 1, cols), lambda l, j: (l, 0, j)),
        ],
        out_specs=pl.BlockSpec((None, ADA_PARTS, BATCH, D_MODEL), lambda l, j: (l, j, 0, 0)),
        out_shape=jax.ShapeDtypeStruct((DEPTH, N_ADA, BATCH, D_MODEL), F32),
        compiler_params=pltpu.CompilerParams(
            dimension_semantics=("arbitrary", "arbitrary"), vmem_limit_bytes=VMEM_LIMIT),
        name="ada_cond",
    )(c, ada_w, ada_b.reshape(DEPTH, 1, N_ADA * D_MODEL))


def _s5_disc_body(lre_ref, lim_ref, ldt_ref, br_ref, bi_ref, are_ref, aim_ref, bbr_ref, bbi_ref):
    lre, lim = lre_ref[...], lim_ref[...]
    dt = jnp.exp(ldt_ref[...])
    mag = jnp.exp(lre * dt)
    ang = lim * dt
    a_re, a_im = mag * jnp.cos(ang), mag * jnp.sin(ang)
    nr, ni = a_re - 1.0, a_im
    den = lre * lre + lim * lim
    k_re = (nr * lre + ni * lim) / den
    k_im = (ni * lre - nr * lim) / den
    br, bi = br_ref[...], bi_ref[...]
    are_ref[...] = a_re
    aim_ref[...] = a_im
    bbr_ref[...] = k_re * br - k_im * bi
    bbi_ref[...] = k_re * bi + k_im * br


def _s5_disc_call(lam_re, lam_im, log_dt, b_re, b_im):
    row = lambda a: a.reshape(DEPTH, 1, N_STATE)
    chan_major = lambda b: jnp.transpose(b, (0, 3, 1, 2)).reshape(DEPTH, S5_GROUP_CH, N_STATE)
    ldt = jnp.repeat(log_dt, S5_STATE, axis=-1)
    vec = pl.BlockSpec((None, 1, N_STATE), lambda l: (l, 0, 0))
    mat = pl.BlockSpec((None, S5_GROUP_CH, N_STATE), lambda l: (l, 0, 0))
    return pl.pallas_call(
        _s5_disc_body,
        grid=(DEPTH,),
        in_specs=[vec, vec, vec, mat, mat],
        out_specs=[vec, vec, mat, mat],
        out_shape=[jax.ShapeDtypeStruct((DEPTH, 1, N_STATE), F32)] * 2
        + [jax.ShapeDtypeStruct((DEPTH, S5_GROUP_CH, N_STATE), F32)] * 2,
        name="s5_discretise",
    )(row(lam_re), row(lam_im), row(ldt), chan_major(b_re), chan_major(b_im))


def _ffn_body(x_ref, mod_ref, g_ref, win_ref, wout_ref, fg_ref, o_ref, *slab, rows_in, rows_out,
              final_norm):
    batch_major = rows_in == "batch"
    x = x_ref[...].reshape(FFN_ROWS, D_MODEL)
    h = _modulate(_rmsnorm(x, g_ref[...]), mod_ref[0], mod_ref[1], batch_major)
    acc = None
    for j in range(D_FF // FFN_CHUNK):
        lo = j * FFN_CHUNK
        a = _dot(h, win_ref[:, lo:lo + FFN_CHUNK])
        b = _dot(h, win_ref[:, D_FF + lo:D_FF + lo + FFN_CHUNK])
        gated = a * jax.nn.sigmoid(a) * b
        part = _dot(gated, wout_ref[lo:lo + FFN_CHUNK, :])
        acc = part if acc is None else acc + part
    y = _gated_residual(x, 0.5 * mod_ref[2], acc, batch_major)
    if final_norm:
        y = _rmsnorm(y, fg_ref[...])
    if rows_in == rows_out:
        o_ref[...] = y.reshape(o_ref.shape)
        return
    (slab,) = slab
    n_slab = D_MODEL // LANES
    lanes = lambda s: slice(s * LANES, (s + 1) * LANES)
    batch_rows = lambda b: pl.ds(b, FFN_STEPS, stride=BATCH)
    if rows_out == "time":
        for s in range(n_slab):
            for b in range(BATCH):
                slab[s, batch_rows(b), :] = y[b * FFN_STEPS:(b + 1) * FFN_STEPS, lanes(s)]
        for s in range(n_slab):
            o_ref[:, lanes(s)] = slab[s]
    else:
        for s in range(n_slab):
            slab[s] = y[:, lanes(s)]
        for b in range(BATCH):
            for s in range(n_slab):
                o_ref[b, :, lanes(s)] = slab[s, batch_rows(b), :]


def _ffn_call(x, cond, g, w_in, w_out, final_g, *, layer, sub, rows_in="time", rows_out="time",
              final_norm=False):
    const = functools.partial(pl.BlockSpec, pipeline_mode=pl.Buffered(1))
    time_rows = pl.BlockSpec((FFN_ROWS, D_MODEL), lambda i: (i, 0))
    batch_rows = pl.BlockSpec((BATCH, FFN_STEPS, D_MODEL), lambda i: (0, i, 0))
    spec = {"time": time_rows, "batch": batch_rows}
    shape = {"time": (N_ROWS, D_MODEL), "batch": (BATCH, SEQ, D_MODEL)}
    reorder = rows_in != rows_out
    return pl.pallas_call(
        functools.partial(_ffn_body, rows_in=rows_in, rows_out=rows_out, final_norm=final_norm),
        grid=(N_ROWS // FFN_ROWS,),
        in_specs=[spec[rows_in],
                  const((None, None, 3, BATCH, D_MODEL), lambda i: (layer, sub, 0, 0, 0)),
                  const((None, 1, D_MODEL), lambda i: (layer, 0, 0)),
                  const((None, D_MODEL, 2 * D_FF), lambda i: (layer, 0, 0)),
                  const((None, D_FF, D_MODEL), lambda i: (layer, 0, 0)),
                  const((1, D_MODEL), lambda i: (0, 0))],
        out_specs=spec[rows_out],
        out_shape=jax.ShapeDtypeStruct(shape[rows_out], F32),
        scratch_shapes=[pltpu.VMEM((D_MODEL // LANES, FFN_ROWS, LANES), F32)] if reorder else [],
        compiler_params=pltpu.CompilerParams(
            dimension_semantics=("arbitrary",), vmem_limit_bytes=VMEM_LIMIT),
        name=f"ffn_{rows_in}_to_{rows_out}" + ("_final" if final_norm else ""),
    )(x, cond, g, w_in, w_out, final_g)


def _group_norm(y, g):
    ms = jnp.mean(y * y, axis=-1, keepdims=True)
    return (y * lax.rsqrt(ms + EPS) * g).astype(BF16)


def _sgu_norm(za, seg_mean, v_hist):
    z = jax.nn.gelu(za)
    u, v = z[:, :W_GRP], z[:, W_GRP:]
    seg = lambda t: _dot(t.astype(BF16), seg_mean)
    d = v - seg(v)
    vn = d * lax.rsqrt(seg(d * d) + EPS)
    for s in range(W_GRP // LANES):
        v_hist[s, MIX_ROWS:, :] = vn[:, s * LANES:(s + 1) * LANES]
    return u


def _sgu_mix(sgu_w, sgu_bias, v_hist, m_slab):
    n_slab = W_GRP // LANES
    lane = lax.broadcasted_iota(jnp.int32, (CHUNK, BATCH * LANES), 1)
    first_head = jnp.bitwise_and(lane, LANES - 1) < SGU_HEAD_DIM
    for s in range(n_slab):
        cols = [v_hist[s, pl.ds(b, CHUNK, stride=BATCH), :] for b in range(BATCH)]
        vt = jnp.concatenate(cols, axis=1)
        rhs = jnp.concatenate([jnp.where(first_head, vt, 0.0), jnp.where(first_head, 0.0, vt)],
                              axis=0)
        mixed = _dot(sgu_w[s], rhs.astype(BF16))
        mixed = mixed + jnp.concatenate([sgu_bias[:, s * LANES:(s + 1) * LANES]] * BATCH, axis=1)
        for b in range(BATCH):
            m_slab[s, pl.ds(b, MIX_STEPS, stride=BATCH), :] = mixed[:, b * LANES:(b + 1) * LANES]
        v_hist[s, :MIX_ROWS, :] = v_hist[s, MIX_ROWS:, :]
    return jnp.concatenate([m_slab[s] for s in range(n_slab)], axis=1)


def _pool(zb, halo, t0, pool_w, pool_scale):
    e = jnp.concatenate([halo, zb], axis=0)
    shifted = lambda a, steps: a[:a.shape[0] - steps * BATCH]
    s2 = e[BATCH:] + shifted(e, 1)
    s4 = s2[2 * BATCH:] + shifted(s2, 2)
    s8 = s4[4 * BATCH:] + shifted(s4, 4)
    s16 = s8[8 * BATCH:] + shifted(s8, 8)
    tail = lambda a: a[a.shape[0] - MIX_ROWS:]
    lane = lax.broadcasted_iota(jnp.int32, (MIX_ROWS, W_GRP), 1)
    group = jnp.right_shift(lane, int(math.log2(POOL_GROUP_DIM)))
    win_sum = jnp.where(group == 0, tail(s2),
                        jnp.where(group == 1, tail(s4), jnp.where(group == 2, tail(s8), tail(s16))))
    window = jnp.left_shift(2, group)
    row = lax.broadcasted_iota(jnp.int32, (MIX_ROWS, W_GRP), 0)
    step = t0 + jnp.right_shift(row, int(math.log2(BATCH)))
    count = jnp.clip(step + 1, 1, window).astype(F32)
    p = win_sum / count - zb
    return _dot(p.astype(BF16), pool_w) * pool_scale


def _conv(zc, halo, conv_w):
    bg, cg, xh = zc[:, :W_GRP], zc[:, W_GRP:2 * W_GRP], zc[:, 2 * W_GRP:]
    y = cg * xh
    e = jnp.concatenate([halo, y], axis=0)
    out = (conv_w[0:1] * e[:MIX_ROWS] + conv_w[1:2] * e[BATCH:BATCH + MIX_ROWS] + conv_w[2:3] * y)
    return bg * out, y[MIX_ROWS - CONV_HALO:]


def _s5_scan(a_re, a_im, state_ref, scan_ref, t_lo, t_hi):
    ar = jnp.broadcast_to(a_re, (BATCH, N_STATE))
    ai = jnp.broadcast_to(a_im, (BATCH, N_STATE))
    st = state_ref[...]
    xr, xi = st[:, :N_STATE], st[:, N_STATE:]
    for t in range(t_lo, t_hi):
        rows = slice(t * BATCH, (t + 1) * BATCH)
        bu = scan_ref[rows, :]
        xr, xi = (ar * xr - ai * xi + bu[:, :N_STATE], ar * xi + ai * xr + bu[:, N_STATE:])
        scan_ref[rows, :] = jnp.concatenate([xr, xi], axis=1)
    state_ref[...] = jnp.concatenate([xr, xi], axis=1)


def _s5_out(zd, states, c_out, d_skip, glu_w, glu_b):
    y = _dot(states.astype(BF16), c_out) + d_skip * zd
    y = jax.nn.gelu(y)
    return y * jax.nn.sigmoid(_dot(y.astype(BF16), glu_w) + glu_b)


def _mix_body(x_ref, mod_ref, g_ref, win_ref, seg_ref, sguw_ref, sgub_ref, poolw_ref, pools_ref,
              convw_ref, are_ref, aim_ref, bin_ref, cout_ref, dskip_ref, gluw_ref, glub_ref,
              mg_ref, wout_ref, o_ref,
              z_even, z_odd, x_even, x_odd, pool_halo, conv_halo, state_ref, scan_ref, v_hist,
              m_slab):
    i = pl.program_id(0)

    @pl.when(i == 0)
    def _():
        for ref in (z_odd, x_odd, pool_halo, conv_halo, state_ref, v_hist):
            ref[...] = jnp.zeros_like(ref)

    def step(z_a, x_a, z_b, x_b, parity_b):
        x = x_ref[...]
        h = _modulate(_rmsnorm(x, g_ref[...]), mod_ref[0], mod_ref[1]).astype(BF16)
        x_a[...] = x

        def project(k):
            cols = slice(k * W_GRP, (k + 1) * W_GRP)
            z_a[:, cols] = _dot(h, win_ref[:, cols])

        mg = mg_ref[...]
        a_re, a_im = are_ref[...], aim_ref[...]
        zd = z_b[:, 6 * W_GRP:]
        scan_ref[...] = _dot(zd.astype(BF16), bin_ref[...])
        project(0)
        u = _sgu_norm(z_b[:, :2 * W_GRP], seg_ref[...], v_hist)
        project(1)
        ya = u * _sgu_mix(sguw_ref[parity_b], sgub_ref[parity_b], v_hist, m_slab)
        na = _group_norm(ya, mg[:, :W_GRP])
        project(2)
        _s5_scan(a_re, a_im, state_ref, scan_ref, 0, MIX_STEPS // 2)
        project(3)
        zb = z_b[:, 2 * W_GRP:3 * W_GRP]
        yb = _pool(zb, pool_halo[...], (i - 1) * MIX_STEPS, poolw_ref[...], pools_ref[...])
        pool_halo[...] = zb[MIX_ROWS - POOL_HALO:]
        nb = _group_norm(yb, mg[:, W_GRP:2 * W_GRP])
        project(4)
        _s5_scan(a_re, a_im, state_ref, scan_ref, MIX_STEPS // 2, MIX_STEPS)
        project(5)
        yc, new_halo = _conv(z_b[:, 3 * W_GRP:6 * W_GRP], conv_halo[...], convw_ref[...])
        conv_halo[...] = new_halo
        nc = _group_norm(yc, mg[:, 2 * W_GRP:3 * W_GRP])
        project(6)
        yd = _s5_out(zd, scan_ref[...], cout_ref[...], dskip_ref[...], gluw_ref[...],
                     glub_ref[...])
        nd = _group_norm(yd, mg[:, 3 * W_GRP:])

        y = jnp.concatenate([na, nb, nc, nd], axis=1)
        out = _dot(y, wout_ref[...])
        o_ref[...] = _gated_residual(x_b[...], mod_ref[2], out)

    @pl.when(i % 2 == 0)
    def _():
        step(z_even, x_even, z_odd, x_odd, parity_b=1)

    @pl.when(i % 2 == 1)
    def _():
        step(z_odd, x_odd, z_even, x_even, parity_b=0)


def _mix_call(x, cond, g, p, *, layer):
    def layer_block(a):
        zeros = (0,) * (a.ndim - 1)
        return pl.BlockSpec((None,) + a.shape[1:], lambda i: (layer,) + zeros,
                            pipeline_mode=pl.Buffered(1))
    n_tiles = N_ROWS // MIX_ROWS
    operands = [g, p["w_in"], p["seg_mean"], p["sgu_w"], p["sgu_bias"], p["pool_w"],
                p["pool_scale"], p["conv_w"], p["a_re"], p["a_im"], p["b_in"], p["c_out"],
                p["d_skip"], p["glu_w"], p["glu_b"], p["mix_g"], p["w_out"]]
    cond_spec = pl.BlockSpec((None, None, 3, BATCH, D_MODEL), lambda i: (layer, 1, 0, 0, 0),
                             pipeline_mode=pl.Buffered(1))
    return pl.pallas_call(
        _mix_body,
        grid=(n_tiles + 1,),
        in_specs=[pl.BlockSpec((MIX_ROWS, D_MODEL), lambda i: (jnp.minimum(i, n_tiles - 1), 0)),
                  cond_spec] + [layer_block(o) for o in operands],
        out_specs=pl.BlockSpec((MIX_ROWS, D_MODEL), lambda i: (jnp.maximum(i - 1, 0), 0)),
        out_shape=jax.ShapeDtypeStruct((N_ROWS, D_MODEL), F32),
        scratch_shapes=[
            pltpu.VMEM((MIX_ROWS, P_IN), F32),
            pltpu.VMEM((MIX_ROWS, P_IN), F32),
            pltpu.VMEM((MIX_ROWS, D_MODEL), F32),
            pltpu.VMEM((MIX_ROWS, D_MODEL), F32),
            pltpu.VMEM((POOL_HALO, W_GRP), F32),
            pltpu.VMEM((CONV_HALO, W_GRP), F32),
            pltpu.VMEM((BATCH, 2 * N_STATE), F32),
            pltpu.VMEM((MIX_ROWS, 2 * N_STATE), F32),
            pltpu.VMEM((W_GRP // LANES, 2 * MIX_ROWS, LANES), F32),
            pltpu.VMEM((W_GRP // LANES, MIX_ROWS, LANES), F32),
        ],
        compiler_params=pltpu.CompilerParams(
            dimension_semantics=("arbitrary",), vmem_limit_bytes=VMEM_LIMIT),
        name="mixers",
    )(x, cond, *operands)


def _diag_mask(n_groups, rows, cols):
    r = lax.broadcasted_iota(jnp.int32, (n_groups * rows, n_groups * cols), 0) // rows
    c = lax.broadcasted_iota(jnp.int32, (n_groups * rows, n_groups * cols), 1) // cols
    return r == c


def _block_diag_rows(m, n_groups):
    rows, cols = m.shape[1], m.shape[2] // n_groups
    return jnp.where(_diag_mask(n_groups, rows, cols), jnp.tile(m, (1, n_groups, 1)), 0.0)


def _block_diag_cols(m, n_groups):
    rows, cols = m.shape[1] // n_groups, m.shape[2]
    return jnp.where(_diag_mask(n_groups, rows, cols), jnp.tile(m, (1, 1, n_groups)), 0.0)


def _mixer_params(w_mix_in, sgu_w, sgu_b, pool_w, pool_scale, conv_w, a_re, a_im, bb_re, bb_im,
                  s5_c_re, s5_c_im, s5_d, s5_glu_w, s5_glu_b, mix_norm_g, w_mix_out):
    tril = jnp.tril(jnp.ones((CHUNK, CHUNK), F32))
    w_s = sgu_w * tril
    first = jnp.concatenate([jnp.zeros_like(w_s[:, :, :MIX_STEPS, :MIX_STEPS]),
                             w_s[:, :, :MIX_STEPS, :MIX_STEPS]], axis=3)
    w_half = jnp.stack([first, w_s[:, :, MIX_STEPS:, :]], axis=1)
    sgu_pair = jnp.concatenate([w_half[:, :, 0::2], w_half[:, :, 1::2]], axis=4).astype(BF16)
    sgu_bias = jnp.repeat(jnp.swapaxes(sgu_b, 1, 2), SGU_HEAD_DIM, axis=2)
    sgu_bias = sgu_bias.reshape(DEPTH, 2, MIX_STEPS, W_GRP)
    seg_mean = jnp.where(_diag_mask(SGU_HEADS, SGU_HEAD_DIM, SGU_HEAD_DIM), 1.0 / SGU_HEAD_DIM, 0.0)
    seg_mean = jnp.broadcast_to(seg_mean.astype(BF16), (DEPTH, W_GRP, W_GRP))
    b_in = jnp.concatenate([_block_diag_rows(bb_re, S5_GROUPS), _block_diag_rows(bb_im, S5_GROUPS)],
                           axis=2).astype(BF16)
    chan_major = lambda c: jnp.swapaxes(c, 1, 2).reshape(DEPTH, S5_GROUP_CH, N_STATE)
    c_out = jnp.concatenate([_block_diag_rows(chan_major(s5_c_re), S5_GROUPS),
                             -_block_diag_rows(chan_major(s5_c_im), S5_GROUPS)], axis=2)
    c_out = jnp.swapaxes(c_out.astype(BF16), 1, 2)
    pool_bd = _block_diag_cols(pool_w.reshape(DEPTH, W_GRP, POOL_GROUP_DIM), len(POOL_WINDOWS))
    row = lambda v: v.reshape(DEPTH, 1, -1)
    return {
        "w_in": w_mix_in.astype(BF16), "seg_mean": seg_mean, "sgu_w": sgu_pair,
        "sgu_bias": sgu_bias, "pool_w": pool_bd.astype(BF16),
        "pool_scale": row(pool_scale), "conv_w": conv_w, "a_re": a_re, "a_im": a_im,
        "b_in": b_in, "c_out": c_out, "d_skip": row(s5_d), "glu_w": s5_glu_w.astype(BF16),
        "glu_b": row(s5_glu_b), "mix_g": row(mix_norm_g), "w_out": w_mix_out.astype(BF16),
    }


def kernel(x, c, ada_w, ada_b, norm1_g, ffn1_w_in, ffn1_w_out, norm2_g, w_mix_in, sgu_w, sgu_b, pool_w, pool_scale, conv_w, s5_lambda_re, s5_lambda_im, s5_b_re, s5_b_im, s5_c_re, s5_c_im, s5_d, s5_log_dt, s5_glu_w, s5_glu_b, mix_norm_g, w_mix_out, norm3_g, ffn2_w_in, ffn2_w_out, final_norm_g):
    cond = _ada_call(c, ada_w, ada_b).reshape(DEPTH, 3, 3, BATCH, D_MODEL)
    a_re, a_im, bb_re, bb_im = _s5_disc_call(s5_lambda_re, s5_lambda_im, s5_log_dt, s5_b_re, s5_b_im)
    p = _mixer_params(w_mix_in, sgu_w, sgu_b, pool_w, pool_scale, conv_w, a_re, a_im, bb_re, bb_im,
                      s5_c_re, s5_c_im, s5_d, s5_glu_w, s5_glu_b, mix_norm_g, w_mix_out)
    row = lambda v: v.reshape(DEPTH, 1, D_MODEL)
    g1, g2, g3 = row(norm1_g), row(norm2_g), row(norm3_g)
    final_g = final_norm_g.reshape(1, D_MODEL)

    xs = x
    for l in range(DEPTH):
        last = l == DEPTH - 1
        xs = _ffn_call(xs, cond, g1, ffn1_w_in, ffn1_w_out, final_g, layer=l, sub=0,
                       rows_in="batch" if l == 0 else "time")
        xs = _mix_call(xs, cond, g2, p, layer=l)
        xs = _ffn_call(xs, cond, g3, ffn2_w_in, ffn2_w_out, final_g, layer=l, sub=2,
                       rows_out="batch" if last else "time", final_norm=last)
    return xs
```

```python
import functools
import math

import jax
import jax.numpy as jnp
from jax import lax
from jax.experimental import pallas as pl
from jax.experimental.pallas import tpu as pltpu

F32 = jnp.float32
BF16 = jnp.bfloat16

D_MODEL = 1024
BATCH = 8
SEQ = 2048
DEPTH = 4
W_GRP = 256
SGU_HEADS = 4
SGU_HEAD_DIM = 64
CHUNK = 128
POOL_WINDOWS = (2, 4, 8, 16)
POOL_GROUP_DIM = 64
S5_GROUP_CH = 16
S5_GROUPS = 16
S5_STATE = 64
N_STATE = S5_GROUPS * S5_STATE
P_IN = 7 * W_GRP
D_FF = 2816
N_ADA = 9
EPS = 1e-6

N_ROWS = SEQ * BATCH
SUBLANES = 8
LANES = 128
VMEM_LIMIT = 56 * 1024 * 1024

FFN_ROWS = 512
FFN_CHUNK = 256
FFN_STEPS = FFN_ROWS // BATCH
MIX_STEPS = CHUNK // 2
MIX_ROWS = MIX_STEPS * BATCH
POOL_HALO = 16 * BATCH
CONV_HALO = 2 * BATCH
ADA_PARTS = 3


def _dot(a, b):
    return jnp.dot(a, b, preferred_element_type=F32)


def _split_bf16(x):
    hi = x.astype(BF16)
    lo = (x - hi.astype(F32)).astype(BF16)
    return hi, lo


def _rmsnorm(x, g):
    ms = jnp.mean(x * x, axis=-1, keepdims=True)
    return x * lax.rsqrt(ms + EPS) * g


def _per_batch(x, fn, batch_major=False):
    rows, c = x.shape
    if batch_major:
        return fn(x.reshape(BATCH, rows // BATCH, c), lambda p: p[:, None, :]).reshape(rows, c)
    return fn(x.reshape(rows // BATCH, BATCH, c), lambda p: p[None]).reshape(rows, c)


def _modulate(y, shift, scale, batch_major=False):
    return _per_batch(y, lambda y3, ex: y3 * ex(1.0 + scale) + ex(shift), batch_major)


def _gated_residual(x, gate, y, batch_major=False):
    return x + _per_batch(y, lambda y3, ex: y3 * ex(gate), batch_major)


def _ada_body(c_ref, w_ref, b_ref, o_ref):
    c = c_ref[...]
    c_act = c * jax.nn.sigmoid(c)
    c_hi, c_lo = _split_bf16(c_act)
    w_hi, w_lo = _split_bf16(w_ref[...])
    cond = _dot(c_hi, w_hi) + (_dot(c_lo, w_hi) + _dot(c_hi, w_lo)) + b_ref[...]
    for k in range(ADA_PARTS):
        o_ref[k] = cond[:, k * D_MODEL:(k + 1) * D_MODEL]


def _ada_call(c, ada_w, ada_b):
    cols = ADA_PARTS * D_MODEL
    return pl.pallas_call(
        _ada_body,
        grid=(DEPTH, N_ADA // ADA_PARTS),
        in_specs=[
            pl.BlockSpec((BATCH, D_MODEL), lambda l, j: (0, 0)),
            pl.BlockSpec((None, D_MODEL, cols), lambda l, j: (l, 0, j)),
            pl.BlockSpec((None, 1, cols), lambda l, j: (l, 0, j)),
        ],
        out_specs=pl.BlockSpec((None, ADA_PARTS, BATCH, D_MODEL), lambda l, j: (l, j, 0, 0)),
        out_shape=jax.ShapeDtypeStruct((DEPTH, N_ADA, BATCH, D_MODEL), F32),
        compiler_params=pltpu.CompilerParams(
            dimension_semantics=("arbitrary", "arbitrary"), vmem_limit_bytes=VMEM_LIMIT),
        name="ada_cond",
    )(c, ada_w, ada_b.reshape(DEPTH, 1, N_ADA * D_MODEL))


def _s5_disc_body(lre_ref, lim_ref, ldt_ref, br_ref, bi_ref, are_ref, aim_ref, bbr_ref, bbi_ref):
    lre, lim = lre_ref[...], lim_ref[...]
    dt = jnp.exp(ldt_ref[...])
    mag = jnp.exp(lre * dt)
    ang = lim * dt
    a_re, a_im = mag * jnp.cos(ang), mag * jnp.sin(ang)
    nr, ni = a_re - 1.0, a_im
    den = lre * lre + lim * lim
    k_re = (nr * lre + ni * lim) / den
    k_im = (ni * lre - nr * lim) / den
    br, bi = br_ref[...], bi_ref[...]
    are_ref[...] = a_re
    aim_ref[...] = a_im
    bbr_ref[...] = k_re * br - k_im * bi
    bbi_ref[...] = k_re * bi + k_im * br


def _s5_disc_call(lam_re, lam_im, log_dt, b_re, b_im):
    row = lambda a: a.reshape(DEPTH, 1, N_STATE)
    chan_major = lambda b: jnp.transpose(b, (0, 3, 1, 2)).reshape(DEPTH, S5_GROUP_CH, N_STATE)
    ldt = jnp.repeat(log_dt, S5_STATE, axis=-1)
    vec = pl.BlockSpec((None, 1, N_STATE), lambda l: (l, 0, 0))
    mat = pl.BlockSpec((None, S5_GROUP_CH, N_STATE), lambda l: (l, 0, 0))
    return pl.pallas_call(
        _s5_disc_body,
        grid=(DEPTH,),
        in_specs=[vec, vec, vec, mat, mat],
        out_specs=[vec, vec, mat, mat],
        out_shape=[jax.ShapeDtypeStruct((DEPTH, 1, N_STATE), F32)] * 2
        + [jax.ShapeDtypeStruct((DEPTH, S5_GROUP_CH, N_STATE), F32)] * 2,
        name="s5_discretise",
    )(row(lam_re), row(lam_im), row(ldt), chan_major(b_re), chan_major(b_im))


def _ffn_body(x_ref, mod_ref, g_ref, win_ref, wout_ref, fg_ref, o_ref, *slab, rows_in, rows_out,
              final_norm):
    batch_major = rows_in == "batch"
    x = x_ref[...].reshape(FFN_ROWS, D_MODEL)
    h = _modulate(_rmsnorm(x, g_ref[...]), mod_ref[0], mod_ref[1], batch_major)
    acc = None
    for j in range(D_FF // FFN_CHUNK):
        lo = j * FFN_CHUNK
        a = _dot(h, win_ref[:, lo:lo + FFN_CHUNK])
        b = _dot(h, win_ref[:, D_FF + lo:D_FF + lo + FFN_CHUNK])
        gated = a * jax.nn.sigmoid(a) * b
        part = _dot(gated, wout_ref[lo:lo + FFN_CHUNK, :])
        acc = part if acc is None else acc + part
    y = _gated_residual(x, 0.5 * mod_ref[2], acc, batch_major)
    if final_norm:
        y = _rmsnorm(y, fg_ref[...])
    if rows_in == rows_out:
        o_ref[...] = y.reshape(o_ref.shape)
        return
    (slab,) = slab
    n_slab = D_MODEL // LANES
    lanes = lambda s: slice(s * LANES, (s + 1) * LANES)
    batch_rows = lambda b: pl.ds(b, FFN_STEPS, stride=BATCH)
    if rows_out == "time":
        for s in range(n_slab):
            for b in range(BATCH):
                slab[s, batch_rows(b), :] = y[b * FFN_STEPS:(b + 1) * FFN_STEPS, lanes(s)]
        for s in range(n_slab):
            o_ref[:, lanes(s)] = slab[s]
    else:
        for s in range(n_slab):
            slab[s] = y[:, lanes(s)]
        for b in range(BATCH):
            for s in range(n_slab):
                o_ref[b, :, lanes(s)] = slab[s, batch_rows(b), :]


def _ffn_call(x, cond, g, w_in, w_out, final_g, *, layer, sub, rows_in="time", rows_out="time",
              final_norm=False):
    const = functools.partial(pl.BlockSpec, pipeline_mode=pl.Buffered(1))
    time_rows = pl.BlockSpec((FFN_ROWS, D_MODEL), lambda i: (i, 0))
    batch_rows = pl.BlockSpec((BATCH, FFN_STEPS, D_MODEL), lambda i: (0, i, 0))
    spec = {"time": time_rows, "batch": batch_rows}
    shape = {"time": (N_ROWS, D_MODEL), "batch": (BATCH, SEQ, D_MODEL)}
    reorder = rows_in != rows_out
    return pl.pallas_call(
        functools.partial(_ffn_body, rows_in=rows_in, rows_out=rows_out, final_norm=final_norm),
        grid=(N_ROWS // FFN_ROWS,),
        in_specs=[spec[rows_in],
                  const((None, None, 3, BATCH, D_MODEL), lambda i: (layer, sub, 0, 0, 0)),
                  const((None, 1, D_MODEL), lambda i: (layer, 0, 0)),
                  const((None, D_MODEL, 2 * D_FF), lambda i: (layer, 0, 0)),
                  const((None, D_FF, D_MODEL), lambda i: (layer, 0, 0)),
                  const((1, D_MODEL), lambda i: (0, 0))],
        out_specs=spec[rows_out],
        out_shape=jax.ShapeDtypeStruct(shape[rows_out], F32),
        scratch_shapes=[pltpu.VMEM((D_MODEL // LANES, FFN_ROWS, LANES), F32)] if reorder else [],
        compiler_params=pltpu.CompilerParams(
            dimension_semantics=("arbitrary",), vmem_limit_bytes=VMEM_LIMIT),
        name=f"ffn_{rows_in}_to_{rows_out}" + ("_final" if final_norm else ""),
    )(x, cond, g, w_in, w_out, final_g)


def _group_norm(y, g):
    ms = jnp.mean(y * y, axis=-1, keepdims=True)
    return (y * lax.rsqrt(ms + EPS) * g).astype(BF16)


def _sgu_gelu(za):
    z = jax.nn.gelu(za)
    return z[:, :W_GRP], z[:, W_GRP:]


def _seg_mean(t, seg_mean):
    return _dot(t.astype(BF16), seg_mean)


def _sgu_mix(sgu_w, sgu_bias, v_hist, m_slab):
    n_slab = W_GRP // LANES
    lane = lax.broadcasted_iota(jnp.int32, (CHUNK, BATCH * LANES), 1)
    first_head = jnp.bitwise_and(lane, LANES - 1) < SGU_HEAD_DIM
    for s in range(n_slab):
        cols = [v_hist[s, pl.ds(b, CHUNK, stride=BATCH), :] for b in range(BATCH)]
        vt = jnp.concatenate(cols, axis=1)
        rhs = jnp.concatenate([jnp.where(first_head, vt, 0.0), jnp.where(first_head, 0.0, vt)],
                              axis=0)
        mixed = _dot(sgu_w[s], rhs.astype(BF16))
        mixed = mixed + jnp.concatenate([sgu_bias[:, s * LANES:(s + 1) * LANES]] * BATCH, axis=1)
        for b in range(BATCH):
            m_slab[s, pl.ds(b, MIX_STEPS, stride=BATCH), :] = mixed[:, b * LANES:(b + 1) * LANES]
        v_hist[s, :MIX_ROWS, :] = v_hist[s, MIX_ROWS:, :]
    return jnp.concatenate([m_slab[s] for s in range(n_slab)], axis=1)


def _pool(zb, halo, t0, pool_w, pool_scale):
    e = jnp.concatenate([halo, zb], axis=0)
    shifted = lambda a, steps: a[:a.shape[0] - steps * BATCH]
    s2 = e[BATCH:] + shifted(e, 1)
    s4 = s2[2 * BATCH:] + shifted(s2, 2)
    s8 = s4[4 * BATCH:] + shifted(s4, 4)
    s16 = s8[8 * BATCH:] + shifted(s8, 8)
    tail = lambda a: a[a.shape[0] - MIX_ROWS:]
    lane = lax.broadcasted_iota(jnp.int32, (MIX_ROWS, W_GRP), 1)
    group = jnp.right_shift(lane, int(math.log2(POOL_GROUP_DIM)))
    win_sum = jnp.where(group == 0, tail(s2),
                        jnp.where(group == 1, tail(s4), jnp.where(group == 2, tail(s8), tail(s16))))
    window = jnp.left_shift(2, group)
    row = lax.broadcasted_iota(jnp.int32, (MIX_ROWS, W_GRP), 0)
    step = t0 + jnp.right_shift(row, int(math.log2(BATCH)))
    count = jnp.clip(step + 1, 1, window).astype(F32)
    p = win_sum / count - zb
    return _dot(p.astype(BF16), pool_w) * pool_scale


def _conv(zc, halo, conv_w):
    bg, cg, xh = zc[:, :W_GRP], zc[:, W_GRP:2 * W_GRP], zc[:, 2 * W_GRP:]
    y = cg * xh
    e = jnp.concatenate([halo, y], axis=0)
    out = (conv_w[0:1] * e[:MIX_ROWS] + conv_w[1:2] * e[BATCH:BATCH + MIX_ROWS] + conv_w[2:3] * y)
    return bg * out, y[MIX_ROWS - CONV_HALO:]


def _s5_scan(a_re, a_im, state_ref, scan_ref, t_lo, t_hi):
    ar = jnp.broadcast_to(a_re, (BATCH, N_STATE))
    ai = jnp.broadcast_to(a_im, (BATCH, N_STATE))
    st = state_ref[...]
    xr, xi = st[:, :N_STATE], st[:, N_STATE:]
    for t in range(t_lo, t_hi):
        rows = slice(t * BATCH, (t + 1) * BATCH)
        bu = scan_ref[rows, :]
        xr, xi = (ar * xr - ai * xi + bu[:, :N_STATE], ar * xi + ai * xr + bu[:, N_STATE:])
        scan_ref[rows, :] = jnp.concatenate([xr, xi], axis=1)
    state_ref[...] = jnp.concatenate([xr, xi], axis=1)


def _s5_gate(y, glu_w, glu_b):
    y = jax.nn.gelu(y)
    return y * jax.nn.sigmoid(_dot(y.astype(BF16), glu_w) + glu_b)


def _mix_body(x_ref, mod_ref, g_ref, win_ref, seg_ref, sguw_ref, sgub_ref, poolw_ref, pools_ref,
              convw_ref, are_ref, aim_ref, bin_ref, cout_ref, dskip_ref, gluw_ref, glub_ref,
              mg_ref, wout_ref, o_ref,
              z_even, z_odd, x_even, x_odd, pool_halo, conv_halo, state_ref, scan_ref, v_hist,
              m_slab):
    i = pl.program_id(0)

    @pl.when(i == 0)
    def _():
        for ref in (z_odd, x_odd, pool_halo, conv_halo, state_ref, v_hist):
            ref[...] = jnp.zeros_like(ref)

    def step(z_a, x_a, z_b, x_b, parity_b):
        x = x_ref[...]
        h = _modulate(_rmsnorm(x, g_ref[...]), mod_ref[0], mod_ref[1]).astype(BF16)
        x_a[...] = x

        def project(k, n=1):
            cols = slice(k * W_GRP, (k + n) * W_GRP)
            z_a[:, cols] = _dot(h, win_ref[:, cols])

        mg = mg_ref[...]
        a_re, a_im = are_ref[...], aim_ref[...]
        zd = z_b[:, 6 * W_GRP:]
        scan_ref[...] = _dot(zd.astype(BF16), bin_ref[...])
        project(0, 2)
        u, v = _sgu_gelu(z_b[:, :2 * W_GRP])
        d = v - _seg_mean(v, seg_ref[...])
        vn = d * lax.rsqrt(_seg_mean(d * d, seg_ref[...]) + EPS)
        for s in range(W_GRP // LANES):
            v_hist[s, MIX_ROWS:, :] = vn[:, s * LANES:(s + 1) * LANES]
        project(2, 2)
        ya = u * _sgu_mix(sguw_ref[parity_b], sgub_ref[parity_b], v_hist, m_slab)
        na = _group_norm(ya, mg[:, :W_GRP])
        project(4)
        _s5_scan(a_re, a_im, state_ref, scan_ref, 0, MIX_STEPS // 2)
        zb = z_b[:, 2 * W_GRP:3 * W_GRP]
        yb = _pool(zb, pool_halo[...], (i - 1) * MIX_STEPS, poolw_ref[...], pools_ref[...])
        pool_halo[...] = zb[MIX_ROWS - POOL_HALO:]
        nb = _group_norm(yb, mg[:, W_GRP:2 * W_GRP])
        _s5_scan(a_re, a_im, state_ref, scan_ref, MIX_STEPS // 2, MIX_STEPS)
        yc, new_halo = _conv(z_b[:, 3 * W_GRP:6 * W_GRP], conv_halo[...], convw_ref[...])
        conv_halo[...] = new_halo
        nc = _group_norm(yc, mg[:, 2 * W_GRP:3 * W_GRP])
        ys = _dot(scan_ref[...].astype(BF16), cout_ref[...]) + dskip_ref[...] * zd
        project(5)
        project(6)
        yd = _s5_gate(ys, gluw_ref[...], glub_ref[...])
        nd = _group_norm(yd, mg[:, 3 * W_GRP:])

        y = jnp.concatenate([na, nb, nc, nd], axis=1)
        out = _dot(y, wout_ref[...])
        o_ref[...] = _gated_residual(x_b[...], mod_ref[2], out)

    @pl.when(i % 2 == 0)
    def _():
        step(z_even, x_even, z_odd, x_odd, parity_b=1)

    @pl.when(i % 2 == 1)
    def _():
        step(z_odd, x_odd, z_even, x_even, parity_b=0)


def _mix_call(x, cond, g, p, *, layer):
    def layer_block(a):
        zeros = (0,) * (a.ndim - 1)
        return pl.BlockSpec((None,) + a.shape[1:], lambda i: (layer,) + zeros,
                            pipeline_mode=pl.Buffered(1))
    n_tiles = N_ROWS // MIX_ROWS
    operands = [g, p["w_in"], p["seg_mean"], p["sgu_w"], p["sgu_bias"], p["pool_w"],
                p["pool_scale"], p["conv_w"], p["a_re"], p["a_im"], p["b_in"], p["c_out"],
                p["d_skip"], p["glu_w"], p["glu_b"], p["mix_g"], p["w_out"]]
    cond_spec = pl.BlockSpec((None, None, 3, BATCH, D_MODEL), lambda i: (layer, 1, 0, 0, 0),
                             pipeline_mode=pl.Buffered(1))
    return pl.pallas_call(
        _mix_body,
        grid=(n_tiles + 1,),
        in_specs=[pl.BlockSpec((MIX_ROWS, D_MODEL), lambda i: (jnp.minimum(i, n_tiles - 1), 0)),
                  cond_spec] + [layer_block(o) for o in operands],
        out_specs=pl.BlockSpec((MIX_ROWS, D_MODEL), lambda i: (jnp.maximum(i - 1, 0), 0)),
        out_shape=jax.ShapeDtypeStruct((N_ROWS, D_MODEL), F32),
        scratch_shapes=[
            pltpu.VMEM((MIX_ROWS, P_IN), F32),
            pltpu.VMEM((MIX_ROWS, P_IN), F32),
            pltpu.VMEM((MIX_ROWS, D_MODEL), F32),
            pltpu.VMEM((MIX_ROWS, D_MODEL), F32),
            pltpu.VMEM((POOL_HALO, W_GRP), F32),
            pltpu.VMEM((CONV_HALO, W_GRP), F32),
            pltpu.VMEM((BATCH, 2 * N_STATE), F32),
            pltpu.VMEM((MIX_ROWS, 2 * N_STATE), F32),
            pltpu.VMEM((W_GRP // LANES, 2 * MIX_ROWS, LANES), F32),
            pltpu.VMEM((W_GRP // LANES, MIX_ROWS, LANES), F32),
        ],
        compiler_params=pltpu.CompilerParams(
            dimension_semantics=("arbitrary",), vmem_limit_bytes=VMEM_LIMIT),
        name="mixers",
    )(x, cond, *operands)


def _diag_mask(n_groups, rows, cols):
    r = lax.broadcasted_iota(jnp.int32, (n_groups * rows, n_groups * cols), 0) // rows
    c = lax.broadcasted_iota(jnp.int32, (n_groups * rows, n_groups * cols), 1) // cols
    return r == c


def _block_diag_rows(m, n_groups):
    rows, cols = m.shape[1], m.shape[2] // n_groups
    return jnp.where(_diag_mask(n_groups, rows, cols), jnp.tile(m, (1, n_groups, 1)), 0.0)


def _block_diag_cols(m, n_groups):
    rows, cols = m.shape[1] // n_groups, m.shape[2]
    return jnp.where(_diag_mask(n_groups, rows, cols), jnp.tile(m, (1, 1, n_groups)), 0.0)


def _mixer_params(w_mix_in, sgu_w, sgu_b, pool_w, pool_scale, conv_w, a_re, a_im, bb_re, bb_im,
                  s5_c_re, s5_c_im, s5_d, s5_glu_w, s5_glu_b, mix_norm_g, w_mix_out):
    tril = jnp.tril(jnp.ones((CHUNK, CHUNK), F32))
    w_s = sgu_w * tril
    first = jnp.concatenate([jnp.zeros_like(w_s[:, :, :MIX_STEPS, :MIX_STEPS]),
                             w_s[:, :, :MIX_STEPS, :MIX_STEPS]], axis=3)
    w_half = jnp.stack([first, w_s[:, :, MIX_STEPS:, :]], axis=1)
    sgu_pair = jnp.concatenate([w_half[:, :, 0::2], w_half[:, :, 1::2]], axis=4).astype(BF16)
    sgu_bias = jnp.repeat(jnp.swapaxes(sgu_b, 1, 2), SGU_HEAD_DIM, axis=2)
    sgu_bias = sgu_bias.reshape(DEPTH, 2, MIX_STEPS, W_GRP)
    seg_mean = jnp.where(_diag_mask(SGU_HEADS, SGU_HEAD_DIM, SGU_HEAD_DIM), 1.0 / SGU_HEAD_DIM, 0.0)
    seg_mean = jnp.broadcast_to(seg_mean.astype(BF16), (DEPTH, W_GRP, W_GRP))
    b_in = jnp.concatenate([_block_diag_rows(bb_re, S5_GROUPS), _block_diag_rows(bb_im, S5_GROUPS)],
                           axis=2).astype(BF16)
    chan_major = lambda c: jnp.swapaxes(c, 1, 2).reshape(DEPTH, S5_GROUP_CH, N_STATE)
    c_out = jnp.concatenate([_block_diag_rows(chan_major(s5_c_re), S5_GROUPS),
                             -_block_diag_rows(chan_major(s5_c_im), S5_GROUPS)], axis=2)
    c_out = jnp.swapaxes(c_out.astype(BF16), 1, 2)
    pool_bd = _block_diag_cols(pool_w.reshape(DEPTH, W_GRP, POOL_GROUP_DIM), len(POOL_WINDOWS))
    row = lambda v: v.reshape(DEPTH, 1, -1)
    return {
        "w_in": w_mix_in.astype(BF16), "seg_mean": seg_mean, "sgu_w": sgu_pair,
        "sgu_bias": sgu_bias, "pool_w": pool_bd.astype(BF16),
        "pool_scale": row(pool_scale), "conv_w": conv_w, "a_re": a_re, "a_im": a_im,
        "b_in": b_in, "c_out": c_out, "d_skip": row(s5_d), "glu_w": s5_glu_w.astype(BF16),
        "glu_b": row(s5_glu_b), "mix_g": row(mix_norm_g), "w_out": w_mix_out.astype(BF16),
    }


def kernel(x, c, ada_w, ada_b, norm1_g, ffn1_w_in, ffn1_w_out, norm2_g, w_mix_in, sgu_w, sgu_b, pool_w, pool_scale, conv_w, s5_lambda_re, s5_lambda_im, s5_b_re, s5_b_im, s5_c_re, s5_c_im, s5_d, s5_log_dt, s5_glu_w, s5_glu_b, mix_norm_g, w_mix_out, norm3_g, ffn2_w_in, ffn2_w_out, final_norm_g):
    cond = _ada_call(c, ada_w, ada_b).reshape(DEPTH, 3, 3, BATCH, D_MODEL)
    a_re, a_im, bb_re, bb_im = _s5_disc_call(s5_lambda_re, s5_lambda_im, s5_log_dt, s5_b_re, s5_b_im)
    p = _mixer_params(w_mix_in, sgu_w, sgu_b, pool_w, pool_scale, conv_w, a_re, a_im, bb_re, bb_im,
                      s5_c_re, s5_c_im, s5_d, s5_glu_w, s5_glu_b, mix_norm_g, w_mix_out)
    row = lambda v: v.reshape(DEPTH, 1, D_MODEL)
    g1, g2, g3 = row(norm1_g), row(norm2_g), row(norm3_g)
    final_g = final_norm_g.reshape(1, D_MODEL)

    xs = x
    for l in range(DEPTH):
        last = l == DEPTH - 1
        xs = _ffn_call(xs, cond, g1, ffn1_w_in, ffn1_w_out, final_g, layer=l, sub=0,
                       rows_in="batch" if l == 0 else "time")
        xs = _mix_call(xs, cond, g2, p, layer=l)
        xs = _ffn_call(xs, cond, g3, ffn2_w_in, ffn2_w_out, final_g, layer=l, sub=2,
                       rows_out="batch" if last else "time", final_norm=last)
    return xs
```
